```python
import jax, jax.numpy as jnp
from jax import lax
import numpy as np

D_MODEL = 1024
BATCH = 8
SEQ = 4096
DEPTH = 2

D_CONV = D_MODEL
CONV_WIDTH = 3
RET_HEADS = 4
RET_QK_DIM = 128
RET_V_DIM = 256
RET_QK = RET_HEADS * RET_QK_DIM
RET_V = RET_HEADS * RET_V_DIM
CHUNK = 128
ROPE_BASE = 10000.0
D_FF = 4 * D_MODEL
EPS = 1e-6
MAX_START = 1024

IN_SIZES = (D_CONV, D_CONV, D_CONV, RET_QK, RET_QK, RET_V, RET_V, D_MODEL, D_MODEL)
D_IN = 3 * D_CONV + 2 * RET_QK + 2 * RET_V + 2 * D_MODEL

kernel_name = "hybrid_conv_retention_gated_block"


def rmsnorm(x, g):
    xf = x.astype(jnp.float32)
    xf = xf * lax.rsqrt(jnp.mean(xf * xf, axis=-1, keepdims=True) + EPS)
    return (xf * g.astype(jnp.float32)).astype(x.dtype)


def rotary(t, positions):
    half = t.shape[-1] // 2
    inv_freq = ROPE_BASE ** (-jnp.arange(half, dtype=jnp.float32) / half)
    ang = positions.astype(jnp.float32)[..., None] * inv_freq
    cos = jnp.cos(ang)[:, :, None, :].astype(t.dtype)
    sin = jnp.sin(ang)[:, :, None, :].astype(t.dtype)
    t1, t2 = t[..., :half], t[..., half:]
    return jnp.concatenate([t1 * cos - t2 * sin, t1 * sin + t2 * cos], axis=-1)


def short_conv_mixer(b, c, u, conv_w, w_out):
    z = c * u
    z = lax.conv_general_dilated(
        z, conv_w[:, None, :].astype(z.dtype), window_strides=(1,),
        padding=[(CONV_WIDTH - 1, 0)], dimension_numbers=("NWC", "WIO", "NWC"),
        feature_group_count=D_CONV)
    return (b * z) @ w_out


def retention_core(q, k, v, positions):
    bsz, seq = q.shape[0], q.shape[1]
    n_chunks = seq // CHUNK
    q = rotary(q, positions) * (RET_QK_DIM ** -0.5)
    k = rotary(k, positions)

    def to_chunks(t):
        return t.astype(jnp.float32).reshape(bsz, n_chunks, CHUNK, RET_HEADS, t.shape[-1]).swapaxes(0, 1)

    qc, kc, vc = to_chunks(q), to_chunks(k), to_chunks(v)
    log_gamma = jnp.log1p(-jnp.exp2(-5.0 - jnp.arange(RET_HEADS, dtype=jnp.float32)))
    pos = jnp.arange(CHUNK, dtype=jnp.float32)
    rel = pos[:, None] - pos[None, :]
    decay_intra = jnp.where(rel >= 0, jnp.exp(log_gamma[:, None, None] * jnp.maximum(rel, 0.0)), 0.0)
    decay_q = jnp.exp((pos[:, None] + 1.0) * log_gamma)
    decay_k = jnp.exp((CHUNK - 1.0 - pos[:, None]) * log_gamma)
    decay_chunk = jnp.exp(CHUNK * log_gamma)

    scores = jnp.einsum("nbqhd,nbkhd->nbhqk", qc, kc) * decay_intra
    o_intra = jnp.einsum("nbhqk,nbkhv->nbqhv", scores, vc)

    def step(state, inp):
        q_i, k_i, v_i = inp
        o_inter = jnp.einsum("bqhd,bhdv->bqhv", q_i, state) * decay_q[None, :, :, None]
        state = state * decay_chunk[None, :, None, None] + jnp.einsum("bkhd,kh,bkhv->bhdv", k_i, decay_k, v_i)
        return state, o_inter

    state0 = jnp.zeros((bsz, RET_HEADS, RET_QK_DIM, RET_V_DIM), jnp.float32)
    _, o_inter = lax.scan(step, state0, (qc, kc, vc))
    return (o_intra + o_inter).swapaxes(0, 1).reshape(bsz, seq, RET_HEADS, RET_V_DIM)


def retention_mixer(q, k, v, g, positions, ret_norm, w_out):
    bsz, seq = q.shape[0], q.shape[1]
    o = retention_core(q.reshape(bsz, seq, RET_HEADS, RET_QK_DIM),
                       k.reshape(bsz, seq, RET_HEADS, RET_QK_DIM),
                       v.reshape(bsz, seq, RET_HEADS, RET_V_DIM), positions)
    mu = jnp.mean(o, axis=-1, keepdims=True)
    var = jnp.mean(jnp.square(o - mu), axis=-1, keepdims=True)
    o = ((o - mu) * lax.rsqrt(var + EPS)).reshape(bsz, seq, RET_V) * ret_norm.astype(jnp.float32)
    o = o.astype(g.dtype)
    return (jax.nn.silu(g) * o) @ w_out


def hybrid_layer(x, positions, norm_mix, w_in, conv_w, w_conv_out, ret_norm, w_ret_out, w_o,
                 norm_mlp, w_up, w_down):
    h = rmsnorm(x, norm_mix)
    proj = h @ w_in
    split_points = np.cumsum(IN_SIZES)[:-1].tolist()
    cb, cc, cu, q, k, v, g, ga, gb = jnp.split(proj, split_points, axis=-1)
    y_conv = short_conv_mixer(cb, cc, cu, conv_w, w_conv_out)
    y_ret = retention_mixer(q, k, v, g, positions, ret_norm, w_ret_out)
    merged = jax.nn.sigmoid(ga) * y_conv + jax.nn.sigmoid(gb) * y_ret
    x = x + merged @ w_o
    h2 = rmsnorm(x, norm_mlp)
    return x + jnp.square(jax.nn.relu(h2 @ w_up)) @ w_down


def setup_inputs(seed: int = 0) -> dict:
    key = jax.random.key(seed)
    ks = jax.random.split(key, 14)
    f32 = jnp.float32

    def dense(k, fan_in, fan_out):
        return jax.random.normal(k, (DEPTH, fan_in, fan_out), f32) * (fan_in ** -0.5)

    def gain(k, shape):
        return 1.0 + 0.02 * jax.random.normal(k, shape, f32)

    x = jax.random.normal(ks[0], (BATCH, SEQ, D_MODEL), f32)
    start = jax.random.randint(ks[1], (BATCH, 1), 0, MAX_START, dtype=jnp.int32)
    positions = (start + jnp.arange(SEQ, dtype=jnp.int32)[None, :]).astype(jnp.int32)
    return {
        "x": x,
        "positions": positions,
        "norm_mix": gain(ks[2], (DEPTH, D_MODEL)),
        "w_in": dense(ks[3], D_MODEL, D_IN),
        "conv_w": jax.random.normal(ks[4], (DEPTH, CONV_WIDTH, D_CONV), f32) * (CONV_WIDTH ** -0.5),
        "w_conv_out": dense(ks[5], D_CONV, D_MODEL),
        "ret_norm": gain(ks[6], (DEPTH, RET_V)),
        "w_ret_out": dense(ks[7], RET_V, D_MODEL),
        "w_o": dense(ks[8], D_MODEL, D_MODEL),
        "norm_mlp": gain(ks[9], (DEPTH, D_MODEL)),
        "w_up": dense(ks[10], D_MODEL, D_FF),
        "w_down": dense(ks[11], D_FF, D_MODEL),
        "norm_final": gain(ks[12], (D_MODEL,)),
    }


def reference(x, positions, norm_mix, w_in, conv_w, w_conv_out, ret_norm, w_ret_out, w_o,
              norm_mlp, w_up, w_down, norm_final):
    for i in range(DEPTH):
        x = hybrid_layer(x, positions, norm_mix[i], w_in[i], conv_w[i], w_conv_out[i], ret_norm[i],
                         w_ret_out[i], w_o[i], norm_mlp[i], w_up[i], w_down[i])
    return rmsnorm(x, norm_final)
```

```python
import functools

import jax
import jax.numpy as jnp
from jax import lax
from jax.experimental import pallas as pl
from jax.experimental.pallas import tpu as pltpu

D_MODEL = 1024
DEPTH = 2
D_CONV = D_MODEL
CONV_WIDTH = 3
RET_HEADS = 4
RET_QK_DIM = 128
RET_V_DIM = 256
RET_QK = RET_HEADS * RET_QK_DIM
RET_V = RET_HEADS * RET_V_DIM
CHUNK = 128
ROPE_BASE = 10000.0
D_FF = 4 * D_MODEL
EPS = 1e-6
D_IN = 3 * D_CONV + 2 * RET_QK + 2 * RET_V + 2 * D_MODEL

OFF_CB = 0
OFF_CC = OFF_CB + D_CONV
OFF_CU = OFF_CC + D_CONV
OFF_Q = OFF_CU + D_CONV
OFF_K = OFF_Q + RET_QK
OFF_V = OFF_K + RET_QK
OFF_G = OFF_V + RET_V
OFF_GA = OFF_G + RET_V
OFF_GB = OFF_GA + D_MODEL

V7X_SUBLANES = 8
TOKEN_TILE = 512
VMEM_LIMIT_BYTES = 56 * 1024 * 1024

F32 = jnp.float32
BF16 = jnp.bfloat16


def _rms(x, g):
    return x * lax.rsqrt(jnp.mean(x * x, axis=-1, keepdims=True) + EPS) * g


def _dot(a, b):
    return jnp.dot(a, b, preferred_element_type=F32)


def _rope_kernel(pos_ref, inv_ref, cos_ref, sin_ref):
    ang = pos_ref[...].astype(F32) * inv_ref[...]
    lane = lax.broadcasted_iota(jnp.int32, ang.shape, 1)
    sin = jnp.sin(ang)
    cos_ref[...] = jnp.cos(ang)
    sin_ref[...] = jnp.where(lane < RET_QK_DIM // 2, -sin, sin)


def _rotate(t, cos_t, sin_t):
    return t * cos_t + pltpu.roll(t, RET_QK_DIM // 2, 1) * sin_t


def _mixer_kernel(x_ref, cos_ref, sin_ref, nm_ref, win_ref, cw_ref, wco_ref, rn_ref, wro_ref,
                  wo_ref, di_ref, dq_ref, dk_ref, dc_ref, o_ref, state_ref, zprev_ref, s_ref):
    tm = x_ref.shape[0]

    @pl.when(pl.program_id(1) == 0)
    def _():
        state_ref[...] = jnp.zeros_like(state_ref)
        zprev_ref[...] = jnp.zeros_like(zprev_ref)

    x = x_ref[...]
    h = _rms(x, nm_ref[...]).astype(BF16)

    def proj(off, size):
        return _dot(h, win_ref[:, off:off + size])

    z = proj(OFF_CC, D_CONV) * proj(OFF_CU, D_CONV)
    zp = zprev_ref[...]
    row = lax.broadcasted_iota(jnp.int32, z.shape, 0)
    z1 = jnp.where(row == 0, zp[V7X_SUBLANES - 1:V7X_SUBLANES], pltpu.roll(z, 1, 0))
    z2 = jnp.where(row == 0, zp[V7X_SUBLANES - 2:V7X_SUBLANES - 1],
                   jnp.where(row == 1, zp[V7X_SUBLANES - 1:V7X_SUBLANES], pltpu.roll(z, 2, 0)))
    zprev_ref[...] = z[tm - V7X_SUBLANES:tm]
    cw = cw_ref[...]
    conv = cw[0:1] * z2 + cw[1:2] * z1 + cw[2:3] * z
    y_conv = _dot((proj(OFF_CB, D_CONV) * conv).astype(BF16), wco_ref[...])

    q = proj(OFF_Q, RET_QK)
    k = proj(OFF_K, RET_QK)
    v = proj(OFF_V, RET_V).astype(BF16)
    g = proj(OFF_G, RET_V)
    gate = g * jax.nn.sigmoid(g) * rn_ref[...]
    cos_t = cos_ref[...]
    sin_t = sin_ref[...]
    scale = RET_QK_DIM ** -0.5
    for c in range(tm // CHUNK):
        rows = slice(c * CHUNK, (c + 1) * CHUNK)
        for hd in range(RET_HEADS):
            qk_cols = slice(hd * RET_QK_DIM, (hd + 1) * RET_QK_DIM)
            v_cols = slice(hd * RET_V_DIM, (hd + 1) * RET_V_DIM)
            qr = _rotate(q[rows, qk_cols], cos_t[rows], sin_t[rows]) * scale
            kr = _rotate(k[rows, qk_cols], cos_t[rows], sin_t[rows])
            vh = v[rows, v_cols]
            krb = kr.astype(BF16)
            scores = lax.dot_general(qr.astype(BF16), krb, (((1,), (1,)), ((), ())),
                                     preferred_element_type=F32) * di_ref[hd]
            st = state_ref[hd]
            o = _dot(scores.astype(BF16), vh) + _dot((qr * dq_ref[hd]).astype(BF16), st.astype(BF16))
            state_ref[hd] = st * dc_ref[hd] + lax.dot_general(
                (kr * dk_ref[hd]).astype(BF16), vh, (((0,), (0,)), ((), ())),
                preferred_element_type=F32)
            mu = jnp.mean(o, axis=-1, keepdims=True)
            d = o - mu
            var = jnp.mean(d * d, axis=-1, keepdims=True)
            s_ref[rows, v_cols] = (d * lax.rsqrt(var + EPS) * gate[rows, v_cols]).astype(BF16)
    y_ret = _dot(s_ref[...], wro_ref[...])

    merged = (jax.nn.sigmoid(proj(OFF_GA, D_MODEL)) * y_conv
              + jax.nn.sigmoid(proj(OFF_GB, D_MODEL)) * y_ret)
    o_ref[...] = x + _dot(merged.astype(BF16), wo_ref[...])


def _mlp_kernel(x_ref, nm_ref, wup_ref, wdn_ref, nf_ref, o_ref, *, final_norm):
    x = x_ref[...]
    h = _rms(x, nm_ref[...]).astype(BF16)
    a = jnp.square(jnp.maximum(_dot(h, wup_ref[...]), 0.0)).astype(BF16)
    y = x + _dot(a, wdn_ref[...])
    if final_norm:
        y = _rms(y, nf_ref[...])
    o_ref[...] = y


def _resident(shape, index_map):
    return pl.BlockSpec(shape, index_map, pipeline_mode=pl.Buffered(1))


def _decay_tables():
    log_gamma = jnp.log1p(-jnp.exp2(-5.0 - jnp.arange(RET_HEADS, dtype=F32)))
    pos = jnp.arange(CHUNK, dtype=F32)
    rel = pos[:, None] - pos[None, :]
    decay_intra = jnp.where(rel >= 0, jnp.exp(log_gamma[:, None, None] * jnp.maximum(rel, 0.0)), 0.0)
    decay_q = jnp.exp((pos[:, None] + 1.0) * log_gamma)
    decay_k = jnp.exp((CHUNK - 1.0 - pos[:, None]) * log_gamma)
    decay_chunk = jnp.exp(CHUNK * log_gamma)
    dq = jnp.broadcast_to(decay_q.T[:, :, None], (RET_HEADS, CHUNK, RET_QK_DIM))
    dk = jnp.broadcast_to(decay_k.T[:, :, None], (RET_HEADS, CHUNK, RET_QK_DIM))
    dc = jnp.broadcast_to(decay_chunk[:, None, None], (RET_HEADS, 1, RET_V_DIM))
    return decay_intra, dq, dk, dc


def _rope_tables(positions):
    bsz, seq = positions.shape
    half = RET_QK_DIM // 2
    inv_freq = ROPE_BASE ** (-jnp.arange(half, dtype=F32) / half)
    inv = jnp.concatenate([inv_freq, inv_freq])[None, :]
    n = bsz * seq
    tm = TOKEN_TILE
    cos_t, sin_t = pl.pallas_call(
        _rope_kernel,
        grid=(n // tm,),
        in_specs=[pl.BlockSpec((tm, 1), lambda i: (i, 0)),
                  pl.BlockSpec((1, RET_QK_DIM), lambda i: (0, 0))],
        out_specs=[pl.BlockSpec((tm, RET_QK_DIM), lambda i: (i, 0))] * 2,
        out_shape=[jax.ShapeDtypeStruct((n, RET_QK_DIM), F32)] * 2,
        name="rope_tables",
    )(positions.reshape(n, 1), inv)
    return cos_t.reshape(bsz, seq, RET_QK_DIM), sin_t.reshape(bsz, seq, RET_QK_DIM)


def _mixer(x, cos_t, sin_t, norm_mix, w_in, conv_w, w_conv_out, ret_norm, w_ret_out, w_o, decays, layer):
    bsz, seq, _ = x.shape
    tm = TOKEN_TILE
    tok = lambda width: pl.BlockSpec((None, tm, width), lambda b, t: (b, t, 0))
    lay = lambda *shape: _resident((None,) + shape, lambda b, t: (layer,) + (0,) * len(shape))
    const = lambda *shape: _resident(shape, lambda b, t: (0,) * len(shape))
    return pl.pallas_call(
        _mixer_kernel,
        grid=(bsz, seq // tm),
        in_specs=[tok(D_MODEL), tok(RET_QK_DIM), tok(RET_QK_DIM),
                  lay(1, D_MODEL), lay(D_MODEL, D_IN), lay(CONV_WIDTH, D_CONV),
                  lay(D_CONV, D_MODEL), lay(1, RET_V), lay(RET_V, D_MODEL), lay(D_MODEL, D_MODEL),
                  const(RET_HEADS, CHUNK, CHUNK), const(RET_HEADS, CHUNK, RET_QK_DIM),
                  const(RET_HEADS, CHUNK, RET_QK_DIM), const(RET_HEADS, 1, RET_V_DIM)],
        out_specs=tok(D_MODEL),
        out_shape=jax.ShapeDtypeStruct(x.shape, F32),
        scratch_shapes=[pltpu.VMEM((RET_HEADS, RET_QK_DIM, RET_V_DIM), F32),
                        pltpu.VMEM((V7X_SUBLANES, D_CONV), F32),
                        pltpu.VMEM((tm, RET_V), BF16)],
        compiler_params=pltpu.CompilerParams(
            dimension_semantics=("arbitrary", "arbitrary"), vmem_limit_bytes=VMEM_LIMIT_BYTES),
        name=f"mixer_l{layer}",
    )(x, cos_t, sin_t, norm_mix, w_in, conv_w, w_conv_out, ret_norm, w_ret_out, w_o, *decays)


def _mlp(x, norm_mlp, w_up, w_down, norm_final, layer, final_norm):
    bsz, seq, _ = x.shape
    n = bsz * seq
    tm = TOKEN_TILE
    lay = lambda *shape: _resident((None,) + shape, lambda i: (layer,) + (0,) * len(shape))
    out = pl.pallas_call(
        functools.partial(_mlp_kernel, final_norm=final_norm),
        grid=(n // tm,),
        in_specs=[pl.BlockSpec((tm, D_MODEL), lambda i: (i, 0)),
                  lay(1, D_MODEL), lay(D_MODEL, D_FF), lay(D_FF, D_MODEL),
                  _resident((1, D_MODEL), lambda i: (0, 0))],
        out_specs=pl.BlockSpec((tm, D_MODEL), lambda i: (i, 0)),
        out_shape=jax.ShapeDtypeStruct((n, D_MODEL), F32),
        compiler_params=pltpu.CompilerParams(
            dimension_semantics=("arbitrary",), vmem_limit_bytes=VMEM_LIMIT_BYTES),
        name=f"mlp_l{layer}",
    )(x.reshape(n, D_MODEL), norm_mlp, w_up, w_down, norm_final)
    return out.reshape(bsz, seq, D_MODEL)


def kernel(x, positions, norm_mix, w_in, conv_w, w_conv_out, ret_norm, w_ret_out, w_o, norm_mlp, w_up, w_down, norm_final):
    assert x.shape[1] % TOKEN_TILE == 0 and TOKEN_TILE % CHUNK == 0
    cos_t, sin_t = _rope_tables(positions)
    decays = _decay_tables()
    as_row = lambda a: a[:, None, :]
    norm_mix, ret_norm, norm_mlp = as_row(norm_mix), as_row(ret_norm), as_row(norm_mlp)
    norm_final = norm_final[None, :]
    w_in, w_conv_out, w_ret_out, w_o, w_up, w_down = (
        w.astype(BF16) for w in (w_in, w_conv_out, w_ret_out, w_o, w_up, w_down))
    for layer in range(DEPTH):
        x = _mixer(x, cos_t, sin_t, norm_mix, w_in, conv_w, w_conv_out, ret_norm, w_ret_out, w_o,
                   decays, layer)
        x = _mlp(x, norm_mlp, w_up, w_down, norm_final, layer, final_norm=layer == DEPTH - 1)
    return x
```

```python
import functools

import jax
import jax.numpy as jnp
from jax import lax
from jax.experimental import pallas as pl
from jax.experimental.pallas import tpu as pltpu

D_MODEL = 1024
DEPTH = 2
D_CONV = D_MODEL
CONV_WIDTH = 3
RET_HEADS = 4
RET_QK_DIM = 128
RET_V_DIM = 256
RET_QK = RET_HEADS * RET_QK_DIM
RET_V = RET_HEADS * RET_V_DIM
CHUNK = 128
ROPE_BASE = 10000.0
D_FF = 4 * D_MODEL
EPS = 1e-6
D_IN = 3 * D_CONV + 2 * RET_QK + 2 * RET_V + 2 * D_MODEL

OFF_CB = 0
OFF_CC = OFF_CB + D_CONV
OFF_CU = OFF_CC + D_CONV
OFF_Q = OFF_CU + D_CONV
OFF_K = OFF_Q + RET_QK
OFF_V = OFF_K + RET_QK
OFF_G = OFF_V + RET_V
OFF_GA = OFF_G + RET_V
OFF_GB = OFF_GA + D_MODEL

V7X_SUBLANES = 8
TOKEN_TILE = 512
VMEM_LIMIT_BYTES = 56 * 1024 * 1024

F32 = jnp.float32
BF16 = jnp.bfloat16


def _rms(x, g):
    return x * lax.rsqrt(jnp.mean(x * x, axis=-1, keepdims=True) + EPS) * g


def _dot(a, b):
    return jnp.dot(a, b, preferred_element_type=F32)


def _sigmoid(x):
    return 0.5 * jnp.tanh(0.5 * x) + 0.5


def _rope_kernel(pos_ref, inv_ref, cos_ref, sin_ref):
    ang = pos_ref[...].astype(F32) * inv_ref[...]
    lane = lax.broadcasted_iota(jnp.int32, ang.shape, 1)
    sin = jnp.sin(ang)
    cos_ref[...] = jnp.cos(ang)
    sin_ref[...] = jnp.where(lane < RET_QK_DIM // 2, -sin, sin)


def _rotate(t, cos_t, sin_t):
    return t * cos_t + pltpu.roll(t, RET_QK_DIM // 2, 1) * sin_t


def _mixer_kernel(x_ref, cos_ref, sin_ref, nm_ref, win_ref, cw_ref, wco_ref, rn_ref, wro_ref,
                  wo_ref, di_ref, dq_ref, dk_ref, dc_ref, o_ref, state_ref, z_ref, h_ref, s_ref):
    tm = x_ref.shape[0]
    n_chunks = tm // CHUNK
    rows = lambda c: slice(c * CHUNK, (c + 1) * CHUNK)
    qk_cols = lambda hd: slice(hd * RET_QK_DIM, (hd + 1) * RET_QK_DIM)
    v_cols = lambda hd: slice(hd * RET_V_DIM, (hd + 1) * RET_V_DIM)
    blocks = [(hd, c) for hd in range(RET_HEADS) for c in range(n_chunks)]

    @pl.when(pl.program_id(1) == 0)
    def _():
        state_ref[...] = jnp.zeros_like(state_ref)
        z_ref[0:V7X_SUBLANES, :] = jnp.zeros((V7X_SUBLANES, D_CONV), F32)

    x = x_ref[...]
    h_ref[...] = _rms(x, nm_ref[...]).astype(BF16)

    def proj(off, size):
        return _dot(h_ref[...], win_ref[:, off:off + size])

    cc = proj(OFF_CC, D_CONV)
    cu = proj(OFF_CU, D_CONV)
    q = proj(OFF_Q, RET_QK)
    k = proj(OFF_K, RET_QK)
    z = cc * cu
    z_ref[V7X_SUBLANES:, :] = z
    cw = cw_ref[...]
    conv = (cw[0:1] * z_ref[V7X_SUBLANES - 2:V7X_SUBLANES - 2 + tm, :]
            + cw[1:2] * z_ref[V7X_SUBLANES - 1:V7X_SUBLANES - 1 + tm, :] + cw[2:3] * z)
    z_ref[0:V7X_SUBLANES, :] = z[tm - V7X_SUBLANES:tm]

    v = proj(OFF_V, RET_V).astype(BF16)
    cos_t = cos_ref[...]
    sin_t = sin_ref[...]

    def rotate_heads(t):
        return jnp.concatenate([_rotate(t[:, qk_cols(hd)], cos_t, sin_t) for hd in range(RET_HEADS)],
                               axis=1)

    qr = rotate_heads(q) * (RET_QK_DIM ** -0.5)
    kr = rotate_heads(k)
    q_intra = qr.astype(BF16)
    q_inter = (qr * dq_ref[...]).astype(BF16)
    k_intra = kr.astype(BF16)
    k_state = (kr * dk_ref[...]).astype(BF16)
    g = proj(OFF_G, RET_V)

    scores = {}
    incr = {}
    for hd, c in blocks:
        scores[hd, c] = (lax.dot_general(q_intra[rows(c), qk_cols(hd)], k_intra[rows(c), qk_cols(hd)],
                                         (((1,), (1,)), ((), ())), preferred_element_type=F32)
                         * di_ref[hd]).astype(BF16)
        incr[hd, c] = lax.dot_general(k_state[rows(c), qk_cols(hd)], v[rows(c), v_cols(hd)],
                                      (((0,), (0,)), ((), ())), preferred_element_type=F32)

    s_conv = (proj(OFF_CB, D_CONV) * conv).astype(BF16)
    state_in = {}
    for hd in range(RET_HEADS):
        st = state_ref[hd]
        for c in range(n_chunks):
            state_in[hd, c] = st.astype(BF16)
            st = st * dc_ref[hd] + incr[hd, c]
        state_ref[hd] = st
    half_g = 0.5 * g
    gate = (half_g * rn_ref[...]) * (jnp.tanh(half_g) + 1.0)

    for hd, c in blocks:
        lhs = jnp.concatenate([scores[hd, c], q_inter[rows(c), qk_cols(hd)]], axis=1)
        rhs = jnp.concatenate([v[rows(c), v_cols(hd)], state_in[hd, c]], axis=0)
        o = _dot(lhs, rhs)
        mu = jnp.mean(o, axis=-1, keepdims=True)
        d = o - mu
        var = jnp.mean(d * d, axis=-1, keepdims=True)
        s_ref[rows(c), v_cols(hd)] = (d * lax.rsqrt(var + EPS) * gate[rows(c), v_cols(hd)]).astype(BF16)
    y_conv = _dot(s_conv, wco_ref[...])
    sig_a_conv = _sigmoid(proj(OFF_GA, D_MODEL)) * y_conv
    y_ret = _dot(s_ref[...], wro_ref[...])
    merged = sig_a_conv + _sigmoid(proj(OFF_GB, D_MODEL)) * y_ret
    o_ref[...] = x + _dot(merged.astype(BF16), wo_ref[...])


def _mlp_kernel(x_ref, nm_ref, wup_ref, wdn_ref, nf_ref, o_ref, *, final_norm):
    x = x_ref[...]
    h = _rms(x, nm_ref[...]).astype(BF16)
    a = jnp.square(jnp.maximum(_dot(h, wup_ref[...]), 0.0)).astype(BF16)
    y = x + _dot(a, wdn_ref[...])
    if final_norm:
        y = _rms(y, nf_ref[...])
    o_ref[...] = y


def _resident(shape, index_map):
    return pl.BlockSpec(shape, index_map, pipeline_mode=pl.Buffered(1))


def _decay_tables():
    log_gamma = jnp.log1p(-jnp.exp2(-5.0 - jnp.arange(RET_HEADS, dtype=F32)))
    pos = jnp.arange(CHUNK, dtype=F32)
    rel = pos[:, None] - pos[None, :]
    decay_intra = jnp.where(rel >= 0, jnp.exp(log_gamma[:, None, None] * jnp.maximum(rel, 0.0)), 0.0)
    decay_q = jnp.exp((pos[:, None] + 1.0) * log_gamma)
    decay_k = jnp.exp((CHUNK - 1.0 - pos[:, None]) * log_gamma)
    decay_chunk = jnp.exp(CHUNK * log_gamma)
    tile_layout = lambda d: jnp.tile(jnp.repeat(d, RET_QK_DIM, axis=1), (TOKEN_TILE // CHUNK, 1))
    dq = tile_layout(decay_q)
    dk = tile_layout(decay_k)
    dc = jnp.broadcast_to(decay_chunk[:, None, None], (RET_HEADS, 1, RET_V_DIM))
    return decay_intra, dq, dk, dc


def _rope_tables(positions):
    bsz, seq = positions.shape
    half = RET_QK_DIM // 2
    inv_freq = ROPE_BASE ** (-jnp.arange(half, dtype=F32) / half)
    inv = jnp.concatenate([inv_freq, inv_freq])[None, :]
    n = bsz * seq
    tm = TOKEN_TILE
    cos_t, sin_t = pl.pallas_call(
        _rope_kernel,
        grid=(n // tm,),
        in_specs=[pl.BlockSpec((tm, 1), lambda i: (i, 0)),
                  pl.BlockSpec((1, RET_QK_DIM), lambda i: (0, 0))],
        out_specs=[pl.BlockSpec((tm, RET_QK_DIM), lambda i: (i, 0))] * 2,
        out_shape=[jax.ShapeDtypeStruct((n, RET_QK_DIM), F32)] * 2,
        name="rope_tables",
    )(positions.reshape(n, 1), inv)
    return cos_t.reshape(bsz, seq, RET_QK_DIM), sin_t.reshape(bsz, seq, RET_QK_DIM)


def _mixer(x, cos_t, sin_t, norm_mix, w_in, conv_w, w_conv_out, ret_norm, w_ret_out, w_o, decays, layer):
    bsz, seq, _ = x.shape
    tm = TOKEN_TILE
    tok = lambda width: pl.BlockSpec((None, tm, width), lambda b, t: (b, t, 0))
    lay = lambda *shape: _resident((None,) + shape, lambda b, t: (layer,) + (0,) * len(shape))
    const = lambda *shape: _resident(shape, lambda b, t: (0,) * len(shape))
    return pl.pallas_call(
        _mixer_kernel,
        grid=(bsz, seq // tm),
        in_specs=[tok(D_MODEL), tok(RET_QK_DIM), tok(RET_QK_DIM),
                  lay(1, D_MODEL), lay(D_MODEL, D_IN), lay(CONV_WIDTH, D_CONV),
                  lay(D_CONV, D_MODEL), lay(1, RET_V), lay(RET_V, D_MODEL), lay(D_MODEL, D_MODEL),
                  const(RET_HEADS, CHUNK, CHUNK), const(tm, RET_QK), const(tm, RET_QK),
                  const(RET_HEADS, 1, RET_V_DIM)],
        out_specs=tok(D_MODEL),
        out_shape=jax.ShapeDtypeStruct(x.shape, F32),
        scratch_shapes=[pltpu.VMEM((RET_HEADS, RET_QK_DIM, RET_V_DIM), F32),
                        pltpu.VMEM((V7X_SUBLANES + tm, D_CONV), F32),
                        pltpu.VMEM((tm, D_MODEL), BF16),
                        pltpu.VMEM((tm, RET_V), BF16)],
        compiler_params=pltpu.CompilerParams(
            dimension_semantics=("arbitrary", "arbitrary"), vmem_limit_bytes=VMEM_LIMIT_BYTES),
        name=f"mixer_l{layer}",
    )(x, cos_t, sin_t, norm_mix, w_in, conv_w, w_conv_out, ret_norm, w_ret_out, w_o, *decays)


def _mlp(x, norm_mlp, w_up, w_down, norm_final, layer, final_norm):
    bsz, seq, _ = x.shape
    n = bsz * seq
    tm = TOKEN_TILE
    lay = lambda *shape: _resident((None,) + shape, lambda i: (layer,) + (0,) * len(shape))
    out = pl.pallas_call(
        functools.partial(_mlp_kernel, final_norm=final_norm),
        grid=(n // tm,),
        in_specs=[pl.BlockSpec((tm, D_MODEL), lambda i: (i, 0)),
                  lay(1, D_MODEL), lay(D_MODEL, D_FF), lay(D_FF, D_MODEL),
                  _resident((1, D_MODEL), lambda i: (0, 0))],
        out_specs=pl.BlockSpec((tm, D_MODEL), lambda i: (i, 0)),
        out_shape=jax.ShapeDtypeStruct((n, D_MODEL), F32),
        compiler_params=pltpu.CompilerParams(
            dimension_semantics=("arbitrary",), vmem_limit_bytes=VMEM_LIMIT_BYTES),
        name=f"mlp_l{layer}",
    )(x.reshape(n, D_MODEL), norm_mlp, w_up, w_down, norm_final)
    return out.reshape(bsz, seq, D_MODEL)


def kernel(x, positions, norm_mix, w_in, conv_w, w_conv_out, ret_norm, w_ret_out, w_o, norm_mlp, w_up, w_down, norm_final):
    assert x.shape[1] % TOKEN_TILE == 0 and TOKEN_TILE % CHUNK == 0
    cos_t, sin_t = _rope_tables(positions)
    decays = _decay_tables()
    as_row = lambda a: a[:, None, :]
    norm_mix, ret_norm, norm_mlp = as_row(norm_mix), as_row(ret_norm), as_row(norm_mlp)
    norm_final = norm_final[None, :]
    w_in, w_conv_out, w_ret_out, w_o, w_up, w_down = (
        w.astype(BF16) for w in (w_in, w_conv_out, w_ret_out, w_o, w_up, w_down))
    for layer in range(DEPTH):
        x = _mixer(x, cos_t, sin_t, norm_mix, w_in, conv_w, w_conv_out, ret_norm, w_ret_out, w_o,
                   decays, layer)
        x = _mlp(x, norm_mlp, w_up, w_down, norm_final, layer, final_norm=layer == DEPTH - 1)
    return x
```

```python
import functools

import jax
import jax.numpy as jnp
from jax import lax
from jax.experimental import pallas as pl
from jax.experimental.pallas import tpu as pltpu

D_MODEL = 1024
DEPTH = 2
D_CONV = D_MODEL
CONV_WIDTH = 3
RET_HEADS = 4
RET_QK_DIM = 128
RET_V_DIM = 256
RET_QK = RET_HEADS * RET_QK_DIM
RET_V = RET_HEADS * RET_V_DIM
CHUNK = 128
ROPE_BASE = 10000.0
D_FF = 4 * D_MODEL
EPS = 1e-6
D_IN = 3 * D_CONV + 2 * RET_QK + 2 * RET_V + 2 * D_MODEL

OFF_CB = 0
OFF_CC = OFF_CB + D_CONV
OFF_CU = OFF_CC + D_CONV
OFF_Q = OFF_CU + D_CONV
OFF_K = OFF_Q + RET_QK
OFF_V = OFF_K + RET_QK
OFF_G = OFF_V + RET_V
OFF_GA = OFF_G + RET_V
OFF_GB = OFF_GA + D_MODEL

V7X_SUBLANES = 8
BF16_SUBLANES = 2 * V7X_SUBLANES
TOKEN_TILE = 512
VMEM_LIMIT_BYTES = 56 * 1024 * 1024

F32 = jnp.float32
BF16 = jnp.bfloat16


def _rms(x, g):
    return x * lax.rsqrt(jnp.mean(x * x, axis=-1, keepdims=True) + EPS) * g


def _dot(a, b):
    return jnp.dot(a, b, preferred_element_type=F32)


def _sigmoid(x):
    return 0.5 * jnp.tanh(0.5 * x) + 0.5


N_CAST_MIXER = 4
N_CAST_MLP = 2


def _cast_rows(src_refs, dst_refs):
    for src, dst in zip(src_refs, dst_refs, strict=True):
        dst[...] = src[...].astype(BF16)


def _rope_kernel(pos_ref, inv_ref, *refs):
    cast_src, (cos_ref, sin_ref), cast_dst = (
        refs[:N_CAST_MIXER], refs[N_CAST_MIXER:N_CAST_MIXER + 2], refs[N_CAST_MIXER + 2:])
    _cast_rows(cast_src, cast_dst)
    ang = pos_ref[...].astype(F32) * inv_ref[...]
    lane = lax.broadcasted_iota(jnp.int32, ang.shape, 1)
    sin = jnp.sin(ang)
    cos_ref[...] = jnp.cos(ang)
    sin_ref[...] = jnp.where(lane < RET_QK_DIM // 2, -sin, sin)


def _rotate(t, cos_t, sin_t):
    return t * cos_t + pltpu.roll(t, RET_QK_DIM // 2, 1) * sin_t


def _mixer_kernel(x_ref, cos_ref, sin_ref, nm_ref, win_ref, cw_ref, wco_ref, rn_ref, wro_ref,
                  wo_ref, di_ref, dq_ref, dk_ref, dc_ref, *refs, tiles_per_seq):
    cast_src, o_ref, cast_dst, (state_ref, z_ref, h_ref, s_ref) = (
        refs[:N_CAST_MLP], refs[N_CAST_MLP], refs[N_CAST_MLP + 1:2 * N_CAST_MLP + 1],
        refs[2 * N_CAST_MLP + 1:])
    _cast_rows(cast_src, cast_dst)
    tm = x_ref.shape[0]
    n_chunks = tm // CHUNK
    rows = lambda c: slice(c * CHUNK, (c + 1) * CHUNK)
    qk_cols = lambda hd: slice(hd * RET_QK_DIM, (hd + 1) * RET_QK_DIM)
    v_cols = lambda hd: slice(hd * RET_V_DIM, (hd + 1) * RET_V_DIM)
    blocks = [(hd, c) for hd in range(RET_HEADS) for c in range(n_chunks)]

    @pl.when(pl.program_id(0) % tiles_per_seq == 0)
    def _():
        state_ref[...] = jnp.zeros_like(state_ref)
        z_ref[0:V7X_SUBLANES, :] = jnp.zeros((V7X_SUBLANES, D_CONV), F32)

    x = x_ref[...]
    h_ref[...] = _rms(x, nm_ref[...]).astype(BF16)

    def proj(off, size):
        return _dot(h_ref[...], win_ref[:, off:off + size])

    cc = proj(OFF_CC, D_CONV)
    cu = proj(OFF_CU, D_CONV)
    q = proj(OFF_Q, RET_QK)
    k = proj(OFF_K, RET_QK)
    z = cc * cu
    z_ref[V7X_SUBLANES:, :] = z
    cw = cw_ref[...]
    conv = (cw[0:1] * z_ref[V7X_SUBLANES - 2:V7X_SUBLANES - 2 + tm, :]
            + cw[1:2] * z_ref[V7X_SUBLANES - 1:V7X_SUBLANES - 1 + tm, :] + cw[2:3] * z)
    z_ref[0:V7X_SUBLANES, :] = z[tm - V7X_SUBLANES:tm]

    v = proj(OFF_V, RET_V).astype(BF16)
    cos_t = cos_ref[...]
    sin_t = sin_ref[...]

    def rotate_heads(t):
        return jnp.concatenate([_rotate(t[:, qk_cols(hd)], cos_t, sin_t) for hd in range(RET_HEADS)],
                               axis=1)

    qr = rotate_heads(q) * (RET_QK_DIM ** -0.5)
    kr = rotate_heads(k)
    q_intra = qr.astype(BF16)
    q_inter = (qr * dq_ref[...]).astype(BF16)
    k_intra = kr.astype(BF16)
    k_state = (kr * dk_ref[...]).astype(BF16)
    g = proj(OFF_G, RET_V)

    scores = {}
    incr = {}
    for hd, c in blocks:
        scores[hd, c] = (lax.dot_general(q_intra[rows(c), qk_cols(hd)], k_intra[rows(c), qk_cols(hd)],
                                         (((1,), (1,)), ((), ())), preferred_element_type=F32)
                         * di_ref[hd]).astype(BF16)
        incr[hd, c] = lax.dot_general(k_state[rows(c), qk_cols(hd)], v[rows(c), v_cols(hd)],
                                      (((0,), (0,)), ((), ())), preferred_element_type=F32)

    s_conv = (proj(OFF_CB, D_CONV) * conv).astype(BF16)
    state_in = {}
    for hd in range(RET_HEADS):
        st = state_ref[hd]
        for c in range(n_chunks):
            state_in[hd, c] = st.astype(BF16)
            st = st * dc_ref[hd] + incr[hd, c]
        state_ref[hd] = st
    half_g = 0.5 * g
    gate = (half_g * rn_ref[...]) * (jnp.tanh(half_g) + 1.0)

    for hd, c in blocks:
        lhs = jnp.concatenate([scores[hd, c], q_inter[rows(c), qk_cols(hd)]], axis=1)
        rhs = jnp.concatenate([v[rows(c), v_cols(hd)], state_in[hd, c]], axis=0)
        o = _dot(lhs, rhs)
        mu = jnp.mean(o, axis=-1, keepdims=True)
        d = o - mu
        var = jnp.mean(d * d, axis=-1, keepdims=True)
        s_ref[rows(c), v_cols(hd)] = (d * lax.rsqrt(var + EPS) * gate[rows(c), v_cols(hd)]).astype(BF16)
    y_conv = _dot(s_conv, wco_ref[...])
    sig_a_conv = _sigmoid(proj(OFF_GA, D_MODEL)) * y_conv
    y_ret = _dot(s_ref[...], wro_ref[...])
    merged = sig_a_conv + _sigmoid(proj(OFF_GB, D_MODEL)) * y_ret
    o_ref[...] = x + _dot(merged.astype(BF16), wo_ref[...])


def _mlp_kernel(x_ref, nm_ref, wup_ref, wdn_ref, nf_ref, *refs, final_norm):
    n_cast = len(refs) // 2
    _cast_rows(refs[:n_cast], refs[n_cast + 1:])
    o_ref = refs[n_cast]
    x = x_ref[...]
    h = _rms(x, nm_ref[...]).astype(BF16)
    a = jnp.square(jnp.maximum(_dot(h, wup_ref[...]), 0.0)).astype(BF16)
    y = x + _dot(a, wdn_ref[...])
    if final_norm:
        y = _rms(y, nf_ref[...])
    o_ref[...] = y


def _resident(shape, index_map):
    return pl.BlockSpec(shape, index_map, pipeline_mode=pl.Buffered(1))


def _decay_tables():
    log_gamma = jnp.log1p(-jnp.exp2(-5.0 - jnp.arange(RET_HEADS, dtype=F32)))
    pos = jnp.arange(CHUNK, dtype=F32)
    rel = pos[:, None] - pos[None, :]
    decay_intra = jnp.where(rel >= 0, jnp.exp(log_gamma[:, None, None] * jnp.maximum(rel, 0.0)), 0.0)
    decay_q = jnp.exp((pos[:, None] + 1.0) * log_gamma)
    decay_k = jnp.exp((CHUNK - 1.0 - pos[:, None]) * log_gamma)
    decay_chunk = jnp.exp(CHUNK * log_gamma)
    tile_layout = lambda d: jnp.tile(jnp.repeat(d, RET_QK_DIM, axis=1), (TOKEN_TILE // CHUNK, 1))
    dq = tile_layout(decay_q)
    dk = tile_layout(decay_k)
    dc = jnp.broadcast_to(decay_chunk[:, None, None], (RET_HEADS, 1, RET_V_DIM))
    return decay_intra, dq, dk, dc


def _cast_job(weights, layer, n_steps):
    in_specs, out_specs, out_shapes = [], [], []
    for w in weights:
        _, n_rows, n_cols = w.shape
        block_rows = n_rows // n_steps
        assert block_rows * n_steps == n_rows and block_rows % BF16_SUBLANES == 0
        in_specs.append(pl.BlockSpec((None, block_rows, n_cols), lambda i: (layer, i, 0)))
        out_specs.append(pl.BlockSpec((block_rows, n_cols), lambda i: (i, 0)))
        out_shapes.append(jax.ShapeDtypeStruct((n_rows, n_cols), BF16))
    return in_specs, out_specs, out_shapes


def _rope_tables(positions, mixer_weights):
    bsz, seq = positions.shape
    half = RET_QK_DIM // 2
    inv_freq = ROPE_BASE ** (-jnp.arange(half, dtype=F32) / half)
    inv = jnp.concatenate([inv_freq, inv_freq])[None, :]
    n = bsz * seq
    tm = TOKEN_TILE
    cast_in, cast_out, cast_shapes = _cast_job(mixer_weights, 0, n // tm)
    cos_t, sin_t, *weights = pl.pallas_call(
        _rope_kernel,
        grid=(n // tm,),
        in_specs=[pl.BlockSpec((tm, 1), lambda i: (i, 0)),
                  pl.BlockSpec((1, RET_QK_DIM), lambda i: (0, 0))] + cast_in,
        out_specs=[pl.BlockSpec((tm, RET_QK_DIM), lambda i: (i, 0))] * 2 + cast_out,
        out_shape=[jax.ShapeDtypeStruct((n, RET_QK_DIM), F32)] * 2 + cast_shapes,
        name="rope_tables",
    )(positions.reshape(n, 1), inv, *mixer_weights)
    return cos_t, sin_t, weights


def _mixer(x, cos_t, sin_t, norm_mix, conv_w, ret_norm, weights, decays, mlp_weights, layer, seq):
    n = x.shape[0]
    tm = TOKEN_TILE
    tok = lambda width: pl.BlockSpec((tm, width), lambda i: (i, 0))
    lay = lambda *shape: _resident((None,) + shape, lambda i: (layer,) + (0,) * len(shape))
    const = lambda *shape: _resident(shape, lambda i: (0,) * len(shape))
    w_in, w_conv_out, w_ret_out, w_o = weights
    cast_in, cast_out, cast_shapes = _cast_job(mlp_weights, layer, n // tm)
    out, *mlp_bf16 = pl.pallas_call(
        functools.partial(_mixer_kernel, tiles_per_seq=seq // tm),
        grid=(n // tm,),
        in_specs=[tok(D_MODEL), tok(RET_QK_DIM), tok(RET_QK_DIM),
                  lay(1, D_MODEL), const(D_MODEL, D_IN), lay(CONV_WIDTH, D_CONV),
                  const(D_CONV, D_MODEL), lay(1, RET_V), const(RET_V, D_MODEL), const(D_MODEL, D_MODEL),
                  const(RET_HEADS, CHUNK, CHUNK), const(tm, RET_QK), const(tm, RET_QK),
                  const(RET_HEADS, 1, RET_V_DIM)] + cast_in,
        out_specs=[tok(D_MODEL)] + cast_out,
        out_shape=[jax.ShapeDtypeStruct(x.shape, F32)] + cast_shapes,
        scratch_shapes=[pltpu.VMEM((RET_HEADS, RET_QK_DIM, RET_V_DIM), F32),
                        pltpu.VMEM((V7X_SUBLANES + tm, D_CONV), F32),
                        pltpu.VMEM((tm, D_MODEL), BF16),
                        pltpu.VMEM((tm, RET_V), BF16)],
        compiler_params=pltpu.CompilerParams(
            dimension_semantics=("arbitrary",), vmem_limit_bytes=VMEM_LIMIT_BYTES),
        name=f"mixer_l{layer}",
    )(x, cos_t, sin_t, norm_mix, w_in, conv_w, w_conv_out, ret_norm, w_ret_out, w_o, *decays,
      *mlp_weights)
    return out, mlp_bf16


def _mlp(x, norm_mlp, weights, norm_final, next_mixer_weights, layer):
    n = x.shape[0]
    tm = TOKEN_TILE
    tok = lambda width: pl.BlockSpec((tm, width), lambda i: (i, 0))
    const = lambda *shape: _resident(shape, lambda i: (0,) * len(shape))
    last = layer == DEPTH - 1
    cast_in, cast_out, cast_shapes = ([], [], []) if last else _cast_job(
        next_mixer_weights, layer + 1, n // tm)
    w_up, w_down = weights
    out, *mixer_bf16 = pl.pallas_call(
        functools.partial(_mlp_kernel, final_norm=last),
        grid=(n // tm,),
        in_specs=[tok(D_MODEL), _resident((None, 1, D_MODEL), lambda i: (layer, 0, 0)),
                  const(D_MODEL, D_FF), const(D_FF, D_MODEL), const(1, D_MODEL)] + cast_in,
        out_specs=[tok(D_MODEL)] + cast_out,
        out_shape=[jax.ShapeDtypeStruct(x.shape, F32)] + cast_shapes,
        compiler_params=pltpu.CompilerParams(
            dimension_semantics=("arbitrary",), vmem_limit_bytes=VMEM_LIMIT_BYTES),
        name=f"mlp_l{layer}",
    )(x, norm_mlp, w_up, w_down, norm_final, *([] if last else next_mixer_weights))
    return out, mixer_bf16


def kernel(x, positions, norm_mix, w_in, conv_w, w_conv_out, ret_norm, w_ret_out, w_o, norm_mlp, w_up, w_down, norm_final):
    bsz, seq, _ = x.shape
    assert seq % TOKEN_TILE == 0 and TOKEN_TILE % CHUNK == 0
    mixer_f32 = (w_in, w_conv_out, w_ret_out, w_o)
    mlp_f32 = (w_up, w_down)
    cos_t, sin_t, mixer_bf16 = _rope_tables(positions, mixer_f32)
    decays = _decay_tables()
    as_row = lambda a: a[:, None, :]
    norm_mix, ret_norm, norm_mlp = as_row(norm_mix), as_row(ret_norm), as_row(norm_mlp)
    norm_final = norm_final[None, :]
    x = x.reshape(bsz * seq, D_MODEL)
    for layer in range(DEPTH):
        x, mlp_bf16 = _mixer(x, cos_t, sin_t, norm_mix, conv_w, ret_norm, mixer_bf16, decays,
                             mlp_f32, layer, seq)
        x, mixer_bf16 = _mlp(x, norm_mlp, mlp_bf16, norm_final, mixer_f32, layer)
    return x.reshape(bsz, seq, D_MODEL)
```

```python
import functools

import jax
import jax.numpy as jnp
from jax import lax
from jax.experimental import pallas as pl
from jax.experimental.pallas import tpu as pltpu

D_MODEL = 1024
DEPTH = 2
D_CONV = D_MODEL
CONV_WIDTH = 3
RET_HEADS = 4
RET_QK_DIM = 128
RET_V_DIM = 256
RET_QK = RET_HEADS * RET_QK_DIM
RET_V = RET_HEADS * RET_V_DIM
CHUNK = 128
ROPE_BASE = 10000.0
D_FF = 4 * D_MODEL
EPS = 1e-6
D_IN = 3 * D_CONV + 2 * RET_QK + 2 * RET_V + 2 * D_MODEL

OFF_CB = 0
OFF_CC = OFF_CB + D_CONV
OFF_CU = OFF_CC + D_CONV
OFF_Q = OFF_CU + D_CONV
OFF_K = OFF_Q + RET_QK
OFF_V = OFF_K + RET_QK
OFF_G = OFF_V + RET_V
OFF_GA = OFF_G + RET_V
OFF_GB = OFF_GA + D_MODEL

V7X_SUBLANES = 8
BF16_SUBLANES = 2 * V7X_SUBLANES
TOKEN_TILE = 512
VMEM_LIMIT_BYTES = 56 * 1024 * 1024

F32 = jnp.float32
BF16 = jnp.bfloat16


def _rms(x, g):
    return x * lax.rsqrt(jnp.mean(x * x, axis=-1, keepdims=True) + EPS) * g


def _dot(a, b):
    return jnp.dot(a, b, preferred_element_type=F32)


def _sigmoid(x):
    return 0.5 * jnp.tanh(0.5 * x) + 0.5


N_CAST_MIXER = 4
N_CAST_MLP = 2


def _cast_rows(src_refs, dst_refs):
    for src, dst in zip(src_refs, dst_refs, strict=True):
        dst[...] = src[...].astype(BF16)


def _rope_kernel(pos_ref, inv_ref, *refs):
    cast_src, (cos_ref, sin_ref), cast_dst = (
        refs[:N_CAST_MIXER], refs[N_CAST_MIXER:N_CAST_MIXER + 2], refs[N_CAST_MIXER + 2:])
    _cast_rows(cast_src, cast_dst)
    ang = pos_ref[...].astype(F32) * inv_ref[...]
    cos = jnp.cos(ang)
    sin = jnp.sin(ang)
    cos_ref[...] = jnp.concatenate([cos, cos], axis=0).T
    sin_ref[...] = jnp.concatenate([-sin, sin], axis=0).T


def _rotate(t, cos_t, sin_t):
    return t * cos_t + pltpu.roll(t, RET_QK_DIM // 2, 1) * sin_t


def _mixer_kernel(x_ref, cos_ref, sin_ref, nm_ref, win_ref, cw_ref, wco_ref, rn_ref, wro_ref,
                  wo_ref, di_ref, dq_ref, dk_ref, dc_ref, *refs, tiles_per_seq):
    cast_src, o_ref, cast_dst, (state_ref, z_ref, xg_ref, h_ref, s_ref) = (
        refs[:N_CAST_MLP], refs[N_CAST_MLP], refs[N_CAST_MLP + 1:2 * N_CAST_MLP + 1],
        refs[2 * N_CAST_MLP + 1:])
    _cast_rows(cast_src, cast_dst)
    tm = x_ref.shape[0]
    n_chunks = tm // CHUNK
    rows = lambda c: slice(c * CHUNK, (c + 1) * CHUNK)
    qk_cols = lambda hd: slice(hd * RET_QK_DIM, (hd + 1) * RET_QK_DIM)
    v_cols = lambda hd: slice(hd * RET_V_DIM, (hd + 1) * RET_V_DIM)
    blocks = [(hd, c) for hd in range(RET_HEADS) for c in range(n_chunks)]

    @pl.when(pl.program_id(0) % tiles_per_seq == 0)
    def _():
        state_ref[...] = jnp.zeros_like(state_ref)
        z_ref[0:V7X_SUBLANES, :] = jnp.zeros((V7X_SUBLANES, D_CONV), F32)

    x = x_ref[...]
    xg = x * nm_ref[...]
    xg_ref[...] = xg.astype(BF16)
    cc = _dot(xg_ref[...], win_ref[:, OFF_CC:OFF_CC + D_CONV])
    cu = _dot(xg_ref[...], win_ref[:, OFF_CU:OFF_CU + D_CONV])
    r = lax.rsqrt(jnp.mean(x * x, axis=-1, keepdims=True) + EPS)
    h_ref[...] = (xg * r).astype(BF16)

    def proj(off, size):
        return _dot(h_ref[...], win_ref[:, off:off + size])

    q = proj(OFF_Q, RET_QK)
    k = proj(OFF_K, RET_QK)
    z = cc * cu * (r * r)
    z_ref[V7X_SUBLANES:, :] = z
    cw = cw_ref[...]
    conv = (cw[0:1] * z_ref[V7X_SUBLANES - 2:V7X_SUBLANES - 2 + tm, :]
            + cw[1:2] * z_ref[V7X_SUBLANES - 1:V7X_SUBLANES - 1 + tm, :] + cw[2:3] * z)
    z_ref[0:V7X_SUBLANES, :] = z[tm - V7X_SUBLANES:tm]

    v = proj(OFF_V, RET_V).astype(BF16)
    cos_t = cos_ref[...]
    sin_t = sin_ref[...]

    def rotate_heads(t):
        return jnp.concatenate([_rotate(t[:, qk_cols(hd)], cos_t, sin_t) for hd in range(RET_HEADS)],
                               axis=1)

    qr = rotate_heads(q) * (RET_QK_DIM ** -0.5)
    kr = rotate_heads(k)
    q_intra = qr.astype(BF16)
    q_inter = (qr * dq_ref[...]).astype(BF16)
    k_intra = kr.astype(BF16)
    k_state = (kr * dk_ref[...]).astype(BF16)
    g = proj(OFF_G, RET_V)

    scores = {}
    incr = {}
    for hd, c in blocks:
        scores[hd, c] = (lax.dot_general(q_intra[rows(c), qk_cols(hd)], k_intra[rows(c), qk_cols(hd)],
                                         (((1,), (1,)), ((), ())), preferred_element_type=F32)
                         * di_ref[hd]).astype(BF16)
        incr[hd, c] = lax.dot_general(k_state[rows(c), qk_cols(hd)], v[rows(c), v_cols(hd)],
                                      (((0,), (0,)), ((), ())), preferred_element_type=F32)

    s_conv = (proj(OFF_CB, D_CONV) * conv).astype(BF16)
    state_in = {}
    for hd in range(RET_HEADS):
        st = state_ref[hd]
        for c in range(n_chunks):
            state_in[hd, c] = st.astype(BF16)
            st = st * dc_ref[hd] + incr[hd, c]
        state_ref[hd] = st
    half_g = 0.5 * g
    gate = (half_g * rn_ref[...]) * (jnp.tanh(half_g) + 1.0)

    for hd, c in blocks:
        lhs = jnp.concatenate([scores[hd, c], q_inter[rows(c), qk_cols(hd)]], axis=1)
        rhs = jnp.concatenate([v[rows(c), v_cols(hd)], state_in[hd, c]], axis=0)
        o = _dot(lhs, rhs)
        mu = jnp.mean(o, axis=-1, keepdims=True)
        d = o - mu
        var = jnp.mean(d * d, axis=-1, keepdims=True)
        s_ref[rows(c), v_cols(hd)] = (d * lax.rsqrt(var + EPS) * gate[rows(c), v_cols(hd)]).astype(BF16)
    y_conv = _dot(s_conv, wco_ref[...])
    sig_a_conv = _sigmoid(proj(OFF_GA, D_MODEL)) * y_conv
    y_ret = _dot(s_ref[...], wro_ref[...])
    merged = sig_a_conv + _sigmoid(proj(OFF_GB, D_MODEL)) * y_ret
    o_ref[...] = x + _dot(merged.astype(BF16), wo_ref[...])


def _mlp_kernel(x_ref, nm_ref, wup_ref, wdn_ref, nf_ref, *refs, final_norm):
    n_cast = len(refs) // 2
    _cast_rows(refs[:n_cast], refs[n_cast + 1:])
    o_ref = refs[n_cast]
    x = x_ref[...]
    u = _dot((x * nm_ref[...]).astype(BF16), wup_ref[...])
    a = jnp.square(jnp.maximum(u, 0.0)).astype(BF16)
    r = lax.rsqrt(jnp.mean(x * x, axis=-1, keepdims=True) + EPS)
    y = x + (r * r) * _dot(a, wdn_ref[...])
    if final_norm:
        y = _rms(y, nf_ref[...])
    o_ref[...] = y


def _resident(shape, index_map):
    return pl.BlockSpec(shape, index_map, pipeline_mode=pl.Buffered(1))


def _decay_tables():
    log_gamma = jnp.log1p(-jnp.exp2(-5.0 - jnp.arange(RET_HEADS, dtype=F32)))
    pos = jnp.arange(CHUNK, dtype=F32)
    rel = pos[:, None] - pos[None, :]
    decay_intra = jnp.where(rel >= 0, jnp.exp(log_gamma[:, None, None] * jnp.maximum(rel, 0.0)), 0.0)
    decay_q = jnp.exp((pos[:, None] + 1.0) * log_gamma)
    decay_k = jnp.exp((CHUNK - 1.0 - pos[:, None]) * log_gamma)
    decay_chunk = jnp.exp(CHUNK * log_gamma)
    tile_layout = lambda d: jnp.tile(jnp.repeat(d, RET_QK_DIM, axis=1), (TOKEN_TILE // CHUNK, 1))
    dq = tile_layout(decay_q)
    dk = tile_layout(decay_k)
    dc = jnp.broadcast_to(decay_chunk[:, None, None], (RET_HEADS, 1, RET_V_DIM))
    return decay_intra, dq, dk, dc


def _cast_job(weights, layer, n_steps):
    in_specs, out_specs, out_shapes = [], [], []
    for w in weights:
        _, n_rows, n_cols = w.shape
        block_rows = n_rows // n_steps
        assert block_rows * n_steps == n_rows and block_rows % BF16_SUBLANES == 0
        in_specs.append(pl.BlockSpec((None, block_rows, n_cols), lambda i: (layer, i, 0)))
        out_specs.append(pl.BlockSpec((block_rows, n_cols), lambda i: (i, 0)))
        out_shapes.append(jax.ShapeDtypeStruct((n_rows, n_cols), BF16))
    return in_specs, out_specs, out_shapes


def _rope_tables(positions, mixer_weights):
    bsz, seq = positions.shape
    half = RET_QK_DIM // 2
    inv_freq = ROPE_BASE ** (-jnp.arange(half, dtype=F32) / half)
    n = bsz * seq
    tm = TOKEN_TILE
    cast_in, cast_out, cast_shapes = _cast_job(mixer_weights, 0, n // tm)
    cos_t, sin_t, *weights = pl.pallas_call(
        _rope_kernel,
        grid=(n // tm,),
        in_specs=[pl.BlockSpec((None, 1, tm), lambda i: (i, 0, 0)),
                  pl.BlockSpec((half, 1), lambda i: (0, 0))] + cast_in,
        out_specs=[pl.BlockSpec((tm, RET_QK_DIM), lambda i: (i, 0))] * 2 + cast_out,
        out_shape=[jax.ShapeDtypeStruct((n, RET_QK_DIM), F32)] * 2 + cast_shapes,
        name="rope_tables",
    )(positions.reshape(n // tm, 1, tm), inv_freq[:, None], *mixer_weights)
    return cos_t, sin_t, weights


def _mixer(x, cos_t, sin_t, norm_mix, conv_w, ret_norm, weights, decays, mlp_weights, layer, seq):
    n = x.shape[0]
    tm = TOKEN_TILE
    tok = lambda width: pl.BlockSpec((tm, width), lambda i: (i, 0))
    lay = lambda *shape: _resident((None,) + shape, lambda i: (layer,) + (0,) * len(shape))
    const = lambda *shape: _resident(shape, lambda i: (0,) * len(shape))
    w_in, w_conv_out, w_ret_out, w_o = weights
    cast_in, cast_out, cast_shapes = _cast_job(mlp_weights, layer, n // tm)
    out, *mlp_bf16 = pl.pallas_call(
        functools.partial(_mixer_kernel, tiles_per_seq=seq // tm),
        grid=(n // tm,),
        in_specs=[tok(D_MODEL), tok(RET_QK_DIM), tok(RET_QK_DIM),
                  lay(1, D_MODEL), const(D_MODEL, D_IN), lay(CONV_WIDTH, D_CONV),
                  const(D_CONV, D_MODEL), lay(1, RET_V), const(RET_V, D_MODEL), const(D_MODEL, D_MODEL),
                  const(RET_HEADS, CHUNK, CHUNK), const(tm, RET_QK), const(tm, RET_QK),
                  const(RET_HEADS, 1, RET_V_DIM)] + cast_in,
        out_specs=[tok(D_MODEL)] + cast_out,
        out_shape=[jax.ShapeDtypeStruct(x.shape, F32)] + cast_shapes,
        scratch_shapes=[pltpu.VMEM((RET_HEADS, RET_QK_DIM, RET_V_DIM), F32),
                        pltpu.VMEM((V7X_SUBLANES + tm, D_CONV), F32),
                        pltpu.VMEM((tm, D_MODEL), BF16),
                        pltpu.VMEM((tm, D_MODEL), BF16),
                        pltpu.VMEM((tm, RET_V), BF16)],
        compiler_params=pltpu.CompilerParams(
            dimension_semantics=("arbitrary",), vmem_limit_bytes=VMEM_LIMIT_BYTES),
        name=f"mixer_l{layer}",
    )(x, cos_t, sin_t, norm_mix, w_in, conv_w, w_conv_out, ret_norm, w_ret_out, w_o, *decays,
      *mlp_weights)
    return out, mlp_bf16


def _mlp(x, norm_mlp, weights, norm_final, next_mixer_weights, layer):
    n = x.shape[0]
    tm = TOKEN_TILE
    tok = lambda width: pl.BlockSpec((tm, width), lambda i: (i, 0))
    const = lambda *shape: _resident(shape, lambda i: (0,) * len(shape))
    last = layer == DEPTH - 1
    cast_in, cast_out, cast_shapes = ([], [], []) if last else _cast_job(
        next_mixer_weights, layer + 1, n // tm)
    w_up, w_down = weights
    out, *mixer_bf16 = pl.pallas_call(
        functools.partial(_mlp_kernel, final_norm=last),
        grid=(n // tm,),
        in_specs=[tok(D_MODEL), _resident((None, 1, D_MODEL), lambda i: (layer, 0, 0)),
                  const(D_MODEL, D_FF), const(D_FF, D_MODEL), const(1, D_MODEL)] + cast_in,
        out_specs=[tok(D_MODEL)] + cast_out,
        out_shape=[jax.ShapeDtypeStruct(x.shape, F32)] + cast_shapes,
        compiler_params=pltpu.CompilerParams(
            dimension_semantics=("arbitrary",), vmem_limit_bytes=VMEM_LIMIT_BYTES),
        name=f"mlp_l{layer}",
    )(x, norm_mlp, w_up, w_down, norm_final, *([] if last else next_mixer_weights))
    return out, mixer_bf16


def kernel(x, positions, norm_mix, w_in, conv_w, w_conv_out, ret_norm, w_ret_out, w_o, norm_mlp, w_up, w_down, norm_final):
    bsz, seq, _ = x.shape
    assert seq % TOKEN_TILE == 0 and TOKEN_TILE % CHUNK == 0
    mixer_f32 = (w_in, w_conv_out, w_ret_out, w_o)
    mlp_f32 = (w_up, w_down)
    cos_t, sin_t, mixer_bf16 = _rope_tables(positions, mixer_f32)
    decays = _decay_tables()
    as_row = lambda a: a[:, None, :]
    norm_mix, ret_norm, norm_mlp = as_row(norm_mix), as_row(ret_norm), as_row(norm_mlp)
    norm_final = norm_final[None, :]
    x = x.reshape(bsz * seq, D_MODEL)
    for layer in range(DEPTH):
        x, mlp_bf16 = _mixer(x, cos_t, sin_t, norm_mix, conv_w, ret_norm, mixer_bf16, decays,
                             mlp_f32, layer, seq)
        x, mixer_bf16 = _mlp(x, norm_mlp, mlp_bf16, norm_final, mixer_f32, layer)
    return x.reshape(bsz, seq, D_MODEL)
```

```python
import functools

import jax
import jax.numpy as jnp
from jax import lax
from jax.experimental import pallas as pl
from jax.experimental.pallas import tpu as pltpu

D_MODEL = 1024
DEPTH = 2
D_CONV = D_MODEL
CONV_WIDTH = 3
RET_HEADS = 4
RET_QK_DIM = 128
RET_V_DIM = 256
RET_QK = RET_HEADS * RET_QK_DIM
RET_V = RET_HEADS * RET_V_DIM
CHUNK = 128
ROPE_BASE = 10000.0
D_FF = 4 * D_MODEL
EPS = 1e-6
D_IN = 3 * D_CONV + 2 * RET_QK + 2 * RET_V + 2 * D_MODEL

OFF_CB = 0
OFF_CC = OFF_CB + D_CONV
OFF_CU = OFF_CC + D_CONV
OFF_Q = OFF_CU + D_CONV
OFF_K = OFF_Q + RET_QK
OFF_V = OFF_K + RET_QK
OFF_G = OFF_V + RET_V
OFF_GA = OFF_G + RET_V
OFF_GB = OFF_GA + D_MODEL

V7X_SUBLANES = 8
BF16_SUBLANES = 2 * V7X_SUBLANES
TOKEN_TILE = 512
ROPE_TILE = 2048
MLP_TILE = 2 * TOKEN_TILE
VMEM_LIMIT_BYTES = 56 * 1024 * 1024

F32 = jnp.float32
BF16 = jnp.bfloat16


def _rms(x, g):
    return x * lax.rsqrt(jnp.mean(x * x, axis=-1, keepdims=True) + EPS) * g


def _dot(a, b):
    return jnp.dot(a, b, preferred_element_type=F32)


def _sigmoid(x):
    return 0.5 * jnp.tanh(0.5 * x) + 0.5


N_CAST_MIXER = 4
N_CAST_MLP = 2


def _cast_rows(src_refs, dst_refs):
    for src, dst in zip(src_refs, dst_refs, strict=True):
        dst[...] = src[...].astype(BF16)


def _rope_kernel(pos_ref, inv_ref, *refs):
    cast_src, (cos_ref, sin_ref), cast_dst = (
        refs[:N_CAST_MIXER], refs[N_CAST_MIXER:N_CAST_MIXER + 2], refs[N_CAST_MIXER + 2:])
    _cast_rows(cast_src, cast_dst)
    ang = pos_ref[...].astype(F32) * inv_ref[...]
    cos = jnp.cos(ang)
    sin = jnp.sin(ang)
    cos_ref[...] = jnp.concatenate([cos, cos], axis=0).T
    sin_ref[...] = jnp.concatenate([-sin, sin], axis=0).T


def _rotate(t, cos_t, sin_t):
    return t * cos_t + pltpu.roll(t, RET_QK_DIM // 2, 1) * sin_t


def _mixer_kernel(x_ref, cos_ref, sin_ref, nm_ref, win_ref, cw_ref, wco_ref, rn_ref, wro_ref,
                  wo_ref, di_ref, dq_ref, dk_ref, dc_ref, *refs, tiles_per_seq):
    cast_src, o_ref, cast_dst, (state_ref, z_ref, xg_ref, h_ref, s_ref) = (
        refs[:N_CAST_MLP], refs[N_CAST_MLP], refs[N_CAST_MLP + 1:2 * N_CAST_MLP + 1],
        refs[2 * N_CAST_MLP + 1:])
    _cast_rows(cast_src, cast_dst)
    tm = x_ref.shape[0]
    n_chunks = tm // CHUNK
    rows = lambda c: slice(c * CHUNK, (c + 1) * CHUNK)
    qk_cols = lambda hd: slice(hd * RET_QK_DIM, (hd + 1) * RET_QK_DIM)
    v_cols = lambda hd: slice(hd * RET_V_DIM, (hd + 1) * RET_V_DIM)
    blocks = [(hd, c) for hd in range(RET_HEADS) for c in range(n_chunks)]

    @pl.when(pl.program_id(0) % tiles_per_seq == 0)
    def _():
        state_ref[...] = jnp.zeros_like(state_ref)
        z_ref[0:V7X_SUBLANES, :] = jnp.zeros((V7X_SUBLANES, D_CONV), F32)

    x = x_ref[...]
    xg = x * nm_ref[...]
    xg_ref[...] = xg.astype(BF16)
    cc = _dot(xg_ref[...], win_ref[:, OFF_CC:OFF_CC + D_CONV])
    cu = _dot(xg_ref[...], win_ref[:, OFF_CU:OFF_CU + D_CONV])
    r = lax.rsqrt(jnp.mean(x * x, axis=-1, keepdims=True) + EPS)
    h_ref[...] = (xg * r).astype(BF16)

    def proj(off, size):
        return _dot(h_ref[...], win_ref[:, off:off + size])

    q = proj(OFF_Q, RET_QK)
    k = proj(OFF_K, RET_QK)
    z = cc * cu * (r * r)
    z_ref[V7X_SUBLANES:, :] = z
    cw = cw_ref[...]
    conv = (cw[0:1] * z_ref[V7X_SUBLANES - 2:V7X_SUBLANES - 2 + tm, :]
            + cw[1:2] * z_ref[V7X_SUBLANES - 1:V7X_SUBLANES - 1 + tm, :] + cw[2:3] * z)
    z_ref[0:V7X_SUBLANES, :] = z[tm - V7X_SUBLANES:tm]

    s_conv = (proj(OFF_CB, D_CONV) * conv).astype(BF16)

    v = proj(OFF_V, RET_V).astype(BF16)
    cos_t = cos_ref[...]
    sin_t = sin_ref[...]

    def rotate_heads(t):
        return jnp.concatenate([_rotate(t[:, qk_cols(hd)], cos_t, sin_t) for hd in range(RET_HEADS)],
                               axis=1)

    qr = rotate_heads(q) * (RET_QK_DIM ** -0.5)
    kr = rotate_heads(k)
    q_intra = qr.astype(BF16)
    q_inter = (qr * dq_ref[...]).astype(BF16)
    k_intra = kr.astype(BF16)
    k_state = (kr * dk_ref[...]).astype(BF16)
    g = proj(OFF_G, RET_V)

    scores = {}
    incr = {}
    for hd, c in blocks:
        scores[hd, c] = (lax.dot_general(q_intra[rows(c), qk_cols(hd)], k_intra[rows(c), qk_cols(hd)],
                                         (((1,), (1,)), ((), ())), preferred_element_type=F32)
                         * di_ref[hd]).astype(BF16)
        incr[hd, c] = lax.dot_general(k_state[rows(c), qk_cols(hd)], v[rows(c), v_cols(hd)],
                                      (((0,), (0,)), ((), ())), preferred_element_type=F32)

    state_in = {}
    for hd in range(RET_HEADS):
        st = state_ref[hd]
        for c in range(n_chunks):
            state_in[hd, c] = st.astype(BF16)
            st = st * dc_ref[hd] + incr[hd, c]
        state_ref[hd] = st
    half_g = 0.5 * g
    gate = (half_g * rn_ref[...]) * (jnp.tanh(half_g) + 1.0)

    for hd, c in blocks:
        lhs = jnp.concatenate([scores[hd, c], q_inter[rows(c), qk_cols(hd)]], axis=1)
        rhs = jnp.concatenate([v[rows(c), v_cols(hd)], state_in[hd, c]], axis=0)
        o = _dot(lhs, rhs)
        mu = jnp.mean(o, axis=-1, keepdims=True)
        d = o - mu
        var = jnp.mean(d * d, axis=-1, keepdims=True)
        s_ref[rows(c), v_cols(hd)] = (d * lax.rsqrt(var + EPS) * gate[rows(c), v_cols(hd)]).astype(BF16)
    y_conv = _dot(s_conv, wco_ref[...])
    sig_a_conv = _sigmoid(proj(OFF_GA, D_MODEL)) * y_conv
    y_ret = _dot(s_ref[...], wro_ref[...])
    merged = sig_a_conv + _sigmoid(proj(OFF_GB, D_MODEL)) * y_ret
    o_ref[...] = x + _dot(merged.astype(BF16), wo_ref[...])


def _mlp_kernel(x_ref, nm_ref, wup_ref, wdn_ref, nf_ref, *refs, final_norm):
    n_cast = len(refs) // 2
    _cast_rows(refs[:n_cast], refs[n_cast + 1:])
    o_ref = refs[n_cast]
    for r0 in range(0, x_ref.shape[0], TOKEN_TILE):
        tile_rows = slice(r0, r0 + TOKEN_TILE)
        x = x_ref[tile_rows, :]
        u = _dot((x * nm_ref[...]).astype(BF16), wup_ref[...])
        a = jnp.square(jnp.maximum(u, 0.0)).astype(BF16)
        r = lax.rsqrt(jnp.mean(x * x, axis=-1, keepdims=True) + EPS)
        y = x + (r * r) * _dot(a, wdn_ref[...])
        if final_norm:
            y = _rms(y, nf_ref[...])
        o_ref[tile_rows, :] = y


def _resident(shape, index_map):
    return pl.BlockSpec(shape, index_map, pipeline_mode=pl.Buffered(1))


def _decay_tables():
    log_gamma = jnp.log1p(-jnp.exp2(-5.0 - jnp.arange(RET_HEADS, dtype=F32)))
    pos = jnp.arange(CHUNK, dtype=F32)
    rel = pos[:, None] - pos[None, :]
    decay_intra = jnp.where(rel >= 0, jnp.exp(log_gamma[:, None, None] * jnp.maximum(rel, 0.0)), 0.0)
    decay_q = jnp.exp((pos[:, None] + 1.0) * log_gamma)
    decay_k = jnp.exp((CHUNK - 1.0 - pos[:, None]) * log_gamma)
    decay_chunk = jnp.exp(CHUNK * log_gamma)
    tile_layout = lambda d: jnp.tile(jnp.repeat(d, RET_QK_DIM, axis=1), (TOKEN_TILE // CHUNK, 1))
    dq = tile_layout(decay_q)
    dk = tile_layout(decay_k)
    dc = jnp.broadcast_to(decay_chunk[:, None, None], (RET_HEADS, 1, RET_V_DIM))
    return decay_intra, dq, dk, dc


def _cast_job(weights, layer, n_steps):
    in_specs, out_specs, out_shapes = [], [], []
    for w in weights:
        _, n_rows, n_cols = w.shape
        block_rows = n_rows // n_steps
        assert block_rows * n_steps == n_rows and block_rows % BF16_SUBLANES == 0
        in_specs.append(pl.BlockSpec((None, block_rows, n_cols), lambda i: (layer, i, 0)))
        out_specs.append(pl.BlockSpec((block_rows, n_cols), lambda i: (i, 0)))
        out_shapes.append(jax.ShapeDtypeStruct((n_rows, n_cols), BF16))
    return in_specs, out_specs, out_shapes


def _rope_tables(positions, mixer_weights):
    bsz, seq = positions.shape
    half = RET_QK_DIM // 2
    with jax.ensure_compile_time_eval():
        inv_freq = (ROPE_BASE ** (-jnp.arange(half, dtype=F32) / half))[:, None]
    n = bsz * seq
    tm = ROPE_TILE
    assert n % tm == 0
    cast_in, cast_out, cast_shapes = _cast_job(mixer_weights, 0, n // tm)
    cos_t, sin_t, *weights = pl.pallas_call(
        _rope_kernel,
        grid=(n // tm,),
        in_specs=[pl.BlockSpec((None, 1, tm), lambda i: (i, 0, 0)),
                  pl.BlockSpec((half, 1), lambda i: (0, 0))] + cast_in,
        out_specs=[pl.BlockSpec((tm, RET_QK_DIM), lambda i: (i, 0))] * 2 + cast_out,
        out_shape=[jax.ShapeDtypeStruct((n, RET_QK_DIM), F32)] * 2 + cast_shapes,
        name="rope_tables",
    )(positions.reshape(n // tm, 1, tm), inv_freq, *mixer_weights)
    return cos_t, sin_t, weights


def _mixer(x, cos_t, sin_t, norm_mix, conv_w, ret_norm, weights, decays, mlp_weights, layer, seq):
    n = x.shape[0]
    tm = TOKEN_TILE
    tok = lambda width: pl.BlockSpec((tm, width), lambda i: (i, 0))
    lay = lambda *shape: _resident((None,) + shape, lambda i: (layer,) + (0,) * len(shape))
    const = lambda *shape: _resident(shape, lambda i: (0,) * len(shape))
    w_in, w_conv_out, w_ret_out, w_o = weights
    cast_in, cast_out, cast_shapes = _cast_job(mlp_weights, layer, n // tm)
    out, *mlp_bf16 = pl.pallas_call(
        functools.partial(_mixer_kernel, tiles_per_seq=seq // tm),
        grid=(n // tm,),
        in_specs=[tok(D_MODEL), tok(RET_QK_DIM), tok(RET_QK_DIM),
                  lay(1, D_MODEL), const(D_MODEL, D_IN), lay(CONV_WIDTH, D_CONV),
                  const(D_CONV, D_MODEL), lay(1, RET_V), const(RET_V, D_MODEL), const(D_MODEL, D_MODEL),
                  const(RET_HEADS, CHUNK, CHUNK), const(tm, RET_QK), const(tm, RET_QK),
                  const(RET_HEADS, 1, RET_V_DIM)] + cast_in,
        out_specs=[tok(D_MODEL)] + cast_out,
        out_shape=[jax.ShapeDtypeStruct(x.shape, F32)] + cast_shapes,
        scratch_shapes=[pltpu.VMEM((RET_HEADS, RET_QK_DIM, RET_V_DIM), F32),
                        pltpu.VMEM((V7X_SUBLANES + tm, D_CONV), F32),
                        pltpu.VMEM((tm, D_MODEL), BF16),
                        pltpu.VMEM((tm, D_MODEL), BF16),
                        pltpu.VMEM((tm, RET_V), BF16)],
        compiler_params=pltpu.CompilerParams(
            dimension_semantics=("arbitrary",), vmem_limit_bytes=VMEM_LIMIT_BYTES),
        name=f"mixer_l{layer}",
    )(x, cos_t, sin_t, norm_mix, w_in, conv_w, w_conv_out, ret_norm, w_ret_out, w_o, *decays,
      *mlp_weights)
    return out, mlp_bf16


def _mlp(x, norm_mlp, weights, norm_final, next_mixer_weights, layer):
    n = x.shape[0]
    tm = MLP_TILE
    assert n % tm == 0
    tok = lambda width: pl.BlockSpec((tm, width), lambda i: (i, 0))
    const = lambda *shape: _resident(shape, lambda i: (0,) * len(shape))
    last = layer == DEPTH - 1
    cast_in, cast_out, cast_shapes = ([], [], []) if last else _cast_job(
        next_mixer_weights, layer + 1, n // tm)
    w_up, w_down = weights
    out, *mixer_bf16 = pl.pallas_call(
        functools.partial(_mlp_kernel, final_norm=last),
        grid=(n // tm,),
        in_specs=[tok(D_MODEL), _resident((None, 1, D_MODEL), lambda i: (layer, 0, 0)),
                  const(D_MODEL, D_FF), const(D_FF, D_MODEL), const(1, D_MODEL)] + cast_in,
        out_specs=[tok(D_MODEL)] + cast_out,
        out_shape=[jax.ShapeDtypeStruct(x.shape, F32)] + cast_shapes,
        compiler_params=pltpu.CompilerParams(
            dimension_semantics=("arbitrary",), vmem_limit_bytes=VMEM_LIMIT_BYTES),
        name=f"mlp_l{layer}",
    )(x, norm_mlp, w_up, w_down, norm_final, *([] if last else next_mixer_weights))
    return out, mixer_bf16


def kernel(x, positions, norm_mix, w_in, conv_w, w_conv_out, ret_norm, w_ret_out, w_o, norm_mlp, w_up, w_down, norm_final):
    bsz, seq, _ = x.shape
    assert seq % TOKEN_TILE == 0 and TOKEN_TILE % CHUNK == 0
    mixer_f32 = (w_in, w_conv_out, w_ret_out, w_o)
    mlp_f32 = (w_up, w_down)
    cos_t, sin_t, mixer_bf16 = _rope_tables(positions, mixer_f32)
    with jax.ensure_compile_time_eval():
        decays = _decay_tables()
    as_row = lambda a: a[:, None, :]
    norm_mix, ret_norm, norm_mlp = as_row(norm_mix), as_row(ret_norm), as_row(norm_mlp)
    norm_final = norm_final[None, :]
    x = x.reshape(bsz * seq, D_MODEL)
    for layer in range(DEPTH):
        x, mlp_bf16 = _mixer(x, cos_t, sin_t, norm_mix, conv_w, ret_norm, mixer_bf16, decays,
                             mlp_f32, layer, seq)
        x, mixer_bf16 = _mlp(x, norm_mlp, mlp_bf16, norm_final, mixer_f32, layer)
    return x.reshape(bsz, seq, D_MODEL)
```

```python
import functools

import jax
import jax.numpy as jnp
import numpy as np
from jax import lax
from jax.experimental import pallas as pl
from jax.experimental.pallas import tpu as pltpu

D_MODEL = 1024
DEPTH = 2
D_CONV = D_MODEL
CONV_WIDTH = 3
RET_HEADS = 4
RET_QK_DIM = 128
RET_V_DIM = 256
RET_QK = RET_HEADS * RET_QK_DIM
RET_V = RET_HEADS * RET_V_DIM
CHUNK = 128
ROPE_BASE = 10000.0
D_FF = 4 * D_MODEL
EPS = 1e-6
D_IN = 3 * D_CONV + 2 * RET_QK + 2 * RET_V + 2 * D_MODEL

OFF_CB = 0
OFF_CC = OFF_CB + D_CONV
OFF_CU = OFF_CC + D_CONV
OFF_Q = OFF_CU + D_CONV
OFF_K = OFF_Q + RET_QK
OFF_V = OFF_K + RET_QK
OFF_G = OFF_V + RET_V
OFF_GA = OFF_G + RET_V
OFF_GB = OFF_GA + D_MODEL

V7X_SUBLANES = 8
BF16_SUBLANES = 2 * V7X_SUBLANES
TOKEN_TILE = 512
ROPE_TILE = 2048
MLP_TILE = 2 * TOKEN_TILE
VMEM_LIMIT_BYTES = 56 * 1024 * 1024

F32 = jnp.float32
BF16 = jnp.bfloat16


def _rms(x, g):
    return x * lax.rsqrt(jnp.mean(x * x, axis=-1, keepdims=True) + EPS) * g


def _dot(a, b):
    return jnp.dot(a, b, preferred_element_type=F32)


def _sigmoid(x):
    return 0.5 * jnp.tanh(0.5 * x) + 0.5


N_CAST_MIXER = 4
N_CAST_MLP = 2


def _cast_rows(src_refs, dst_refs):
    for src, dst in zip(src_refs, dst_refs, strict=True):
        dst[...] = src[...].astype(BF16)


def _rope_kernel(pos_ref, inv_ref, *refs):
    cast_src, (cos_ref, sin_ref), cast_dst = (
        refs[:N_CAST_MIXER], refs[N_CAST_MIXER:N_CAST_MIXER + 2], refs[N_CAST_MIXER + 2:])
    _cast_rows(cast_src, cast_dst)
    ang = pos_ref[...].astype(F32) * inv_ref[...]
    cos = jnp.cos(ang)
    sin = jnp.sin(ang)
    cos_ref[...] = jnp.concatenate([cos, cos], axis=0).T
    sin_ref[...] = jnp.concatenate([-sin, sin], axis=0).T


def _rotate(t, cos_t, sin_t):
    return t * cos_t + pltpu.roll(t, RET_QK_DIM // 2, 1) * sin_t


def _mixer_kernel(x_ref, cos_ref, sin_ref, nm_ref, win_ref, cw_ref, wco_ref, rn_ref, wro_ref,
                  wo_ref, di_ref, dq_ref, dk_ref, dc_ref, *refs, tiles_per_seq):
    cast_src, o_ref, cast_dst, (state_ref, z_ref, xg_ref, h_ref, s_ref) = (
        refs[:N_CAST_MLP], refs[N_CAST_MLP], refs[N_CAST_MLP + 1:2 * N_CAST_MLP + 1],
        refs[2 * N_CAST_MLP + 1:])
    _cast_rows(cast_src, cast_dst)
    tm = x_ref.shape[0]
    n_chunks = tm // CHUNK
    rows = lambda c: slice(c * CHUNK, (c + 1) * CHUNK)
    qk_cols = lambda hd: slice(hd * RET_QK_DIM, (hd + 1) * RET_QK_DIM)
    v_cols = lambda hd: slice(hd * RET_V_DIM, (hd + 1) * RET_V_DIM)
    blocks = [(hd, c) for hd in range(RET_HEADS) for c in range(n_chunks)]

    @pl.when(pl.program_id(0) % tiles_per_seq == 0)
    def _():
        state_ref[...] = jnp.zeros_like(state_ref)
        z_ref[...] = jnp.zeros_like(z_ref)

    x = x_ref[...]
    xg = x * nm_ref[...]
    xg_ref[...] = xg.astype(BF16)
    cc = _dot(xg_ref[...], win_ref[:, OFF_CC:OFF_CC + D_CONV])
    cu = _dot(xg_ref[...], win_ref[:, OFF_CU:OFF_CU + D_CONV])
    r = lax.rsqrt(jnp.mean(x * x, axis=-1, keepdims=True) + EPS)
    h_ref[...] = (xg * r).astype(BF16)

    def proj(off, size):
        return _dot(h_ref[...], win_ref[:, off:off + size])

    q = proj(OFF_Q, RET_QK)
    k = proj(OFF_K, RET_QK)
    z = cc * cu * (r * r)
    z_head = jnp.concatenate([z_ref[...], z[0:V7X_SUBLANES]], axis=0)
    z_ref[...] = z[tm - V7X_SUBLANES:tm]
    cw = cw_ref[...]

    def conv3(z0):
        return cw[0:1] * pltpu.roll(z0, 2, 0) + cw[1:2] * pltpu.roll(z0, 1, 0) + cw[2:3] * z0

    conv = jnp.concatenate([conv3(z_head)[V7X_SUBLANES:], conv3(z)[V7X_SUBLANES:]], axis=0)

    v = proj(OFF_V, RET_V).astype(BF16)
    cos_t = cos_ref[...]
    sin_t = sin_ref[...]

    def rotate_heads(t):
        return jnp.concatenate([_rotate(t[:, qk_cols(hd)], cos_t, sin_t) for hd in range(RET_HEADS)],
                               axis=1)

    qr = rotate_heads(q) * (RET_QK_DIM ** -0.5)
    kr = rotate_heads(k)
    q_intra = qr.astype(BF16)
    q_inter = (qr * dq_ref[...]).astype(BF16)
    k_intra = kr.astype(BF16)
    k_state = (kr * dk_ref[...]).astype(BF16)
    g = proj(OFF_G, RET_V)

    scores = {}
    incr = {}
    for hd, c in blocks:
        scores[hd, c] = (lax.dot_general(q_intra[rows(c), qk_cols(hd)], k_intra[rows(c), qk_cols(hd)],
                                         (((1,), (1,)), ((), ())), preferred_element_type=F32)
                         * di_ref[hd]).astype(BF16)
        incr[hd, c] = lax.dot_general(k_state[rows(c), qk_cols(hd)], v[rows(c), v_cols(hd)],
                                      (((0,), (0,)), ((), ())), preferred_element_type=F32)

    s_conv = (proj(OFF_CB, D_CONV) * conv).astype(BF16)
    state_in = {}
    for hd in range(RET_HEADS):
        st = state_ref[hd]
        for c in range(n_chunks):
            state_in[hd, c] = st.astype(BF16)
            st = st * dc_ref[hd] + incr[hd, c]
        state_ref[hd] = st
    half_g = 0.5 * g
    gate = (half_g * rn_ref[...]) * (jnp.tanh(half_g) + 1.0)

    for hd, c in blocks:
        lhs = jnp.concatenate([scores[hd, c], q_inter[rows(c), qk_cols(hd)]], axis=1)
        rhs = jnp.concatenate([v[rows(c), v_cols(hd)], state_in[hd, c]], axis=0)
        o = _dot(lhs, rhs)
        mu = jnp.mean(o, axis=-1, keepdims=True)
        d = o - mu
        var = jnp.mean(d * d, axis=-1, keepdims=True)
        s_ref[rows(c), v_cols(hd)] = (d * lax.rsqrt(var + EPS) * gate[rows(c), v_cols(hd)]).astype(BF16)
    y_conv = _dot(s_conv, wco_ref[...])
    sig_a_conv = _sigmoid(proj(OFF_GA, D_MODEL)) * y_conv
    y_ret = _dot(s_ref[...], wro_ref[...])
    merged = sig_a_conv + _sigmoid(proj(OFF_GB, D_MODEL)) * y_ret
    o_ref[...] = x + _dot(merged.astype(BF16), wo_ref[...])


def _mlp_kernel(x_ref, nm_ref, wup_ref, wdn_ref, nf_ref, *refs, final_norm):
    n_cast = len(refs) // 2
    _cast_rows(refs[:n_cast], refs[n_cast + 1:])
    o_ref = refs[n_cast]
    for r0 in range(0, x_ref.shape[0], TOKEN_TILE):
        tile_rows = slice(r0, r0 + TOKEN_TILE)
        x = x_ref[tile_rows, :]
        u = _dot((x * nm_ref[...]).astype(BF16), wup_ref[...])
        a = jnp.square(jnp.maximum(u, 0.0)).astype(BF16)
        r = lax.rsqrt(jnp.mean(x * x, axis=-1, keepdims=True) + EPS)
        y = x + (r * r) * _dot(a, wdn_ref[...])
        if final_norm:
            y = _rms(y, nf_ref[...])
        o_ref[tile_rows, :] = y


def _resident(shape, index_map):
    return pl.BlockSpec(shape, index_map, pipeline_mode=pl.Buffered(1))


def _decay_tables():
    f32 = np.float32
    log_gamma = np.log1p(-np.exp2(f32(-5.0) - np.arange(RET_HEADS, dtype=f32)))
    pos = np.arange(CHUNK, dtype=f32)
    rel = pos[:, None] - pos[None, :]
    decay_intra = np.where(rel >= 0, np.exp(log_gamma[:, None, None] * np.maximum(rel, f32(0.0))),
                           f32(0.0))
    decay_q = np.exp((pos[:, None] + f32(1.0)) * log_gamma)
    decay_k = np.exp((f32(CHUNK - 1.0) - pos[:, None]) * log_gamma)
    decay_chunk = np.exp(f32(CHUNK) * log_gamma)
    tile_layout = lambda d: np.tile(np.repeat(d, RET_QK_DIM, axis=1), (TOKEN_TILE // CHUNK, 1))
    dc = np.broadcast_to(decay_chunk[:, None, None], (RET_HEADS, 1, RET_V_DIM))
    tables = (decay_intra, tile_layout(decay_q), tile_layout(decay_k), dc)
    assert all(t.dtype == f32 for t in tables)
    return tuple(jnp.asarray(t) for t in tables)


def _cast_job(weights, layer, n_steps):
    in_specs, out_specs, out_shapes = [], [], []
    for w in weights:
        _, n_rows, n_cols = w.shape
        block_rows = n_rows // n_steps
        assert block_rows * n_steps == n_rows and block_rows % BF16_SUBLANES == 0
        in_specs.append(pl.BlockSpec((None, block_rows, n_cols), lambda i: (layer, i, 0)))
        out_specs.append(pl.BlockSpec((block_rows, n_cols), lambda i: (i, 0)))
        out_shapes.append(jax.ShapeDtypeStruct((n_rows, n_cols), BF16))
    return in_specs, out_specs, out_shapes


def _rope_tables(positions, mixer_weights):
    bsz, seq = positions.shape
    half = RET_QK_DIM // 2
    inv_freq = (ROPE_BASE ** (-jnp.arange(half, dtype=F32) / half))[:, None]
    n = bsz * seq
    tm = ROPE_TILE
    assert n % tm == 0
    cast_in, cast_out, cast_shapes = _cast_job(mixer_weights, 0, n // tm)
    cos_t, sin_t, *weights = pl.pallas_call(
        _rope_kernel,
        grid=(n // tm,),
        in_specs=[pl.BlockSpec((None, 1, tm), lambda i: (i, 0, 0)),
                  pl.BlockSpec((half, 1), lambda i: (0, 0))] + cast_in,
        out_specs=[pl.BlockSpec((tm, RET_QK_DIM), lambda i: (i, 0))] * 2 + cast_out,
        out_shape=[jax.ShapeDtypeStruct((n, RET_QK_DIM), F32)] * 2 + cast_shapes,
        name="rope_tables",
    )(positions.reshape(n // tm, 1, tm), inv_freq, *mixer_weights)
    return cos_t, sin_t, weights


def _mixer(x, cos_t, sin_t, norm_mix, conv_w, ret_norm, weights, decays, mlp_weights, layer, seq):
    n = x.shape[0]
    tm = TOKEN_TILE
    tok = lambda width: pl.BlockSpec((tm, width), lambda i: (i, 0))
    lay = lambda *shape: _resident((None,) + shape, lambda i: (layer,) + (0,) * len(shape))
    const = lambda *shape: _resident(shape, lambda i: (0,) * len(shape))
    w_in, w_conv_out, w_ret_out, w_o = weights
    cast_in, cast_out, cast_shapes = _cast_job(mlp_weights, layer, n // tm)
    out, *mlp_bf16 = pl.pallas_call(
        functools.partial(_mixer_kernel, tiles_per_seq=seq // tm),
        grid=(n // tm,),
        in_specs=[tok(D_MODEL), tok(RET_QK_DIM), tok(RET_QK_DIM),
                  lay(1, D_MODEL), const(D_MODEL, D_IN), lay(CONV_WIDTH, D_CONV),
                  const(D_CONV, D_MODEL), lay(1, RET_V), const(RET_V, D_MODEL), const(D_MODEL, D_MODEL),
                  const(RET_HEADS, CHUNK, CHUNK), const(tm, RET_QK), const(tm, RET_QK),
                  const(RET_HEADS, 1, RET_V_DIM)] + cast_in,
        out_specs=[tok(D_MODEL)] + cast_out,
        out_shape=[jax.ShapeDtypeStruct(x.shape, F32)] + cast_shapes,
        scratch_shapes=[pltpu.VMEM((RET_HEADS, RET_QK_DIM, RET_V_DIM), F32),
                        pltpu.VMEM((V7X_SUBLANES, D_CONV), F32),
                        pltpu.VMEM((tm, D_MODEL), BF16),
                        pltpu.VMEM((tm, D_MODEL), BF16),
                        pltpu.VMEM((tm, RET_V), BF16)],
        compiler_params=pltpu.CompilerParams(
            dimension_semantics=("arbitrary",), vmem_limit_bytes=VMEM_LIMIT_BYTES),
        name=f"mixer_l{layer}",
    )(x, cos_t, sin_t, norm_mix, w_in, conv_w, w_conv_out, ret_norm, w_ret_out, w_o, *decays,
      *mlp_weights)
    return out, mlp_bf16


def _mlp(x, norm_mlp, weights, norm_final, next_mixer_weights, layer):
    n = x.shape[0]
    tm = MLP_TILE
    assert n % tm == 0
    tok = lambda width: pl.BlockSpec((tm, width), lambda i: (i, 0))
    const = lambda *shape: _resident(shape, lambda i: (0,) * len(shape))
    last = layer == DEPTH - 1
    cast_in, cast_out, cast_shapes = ([], [], []) if last else _cast_job(
        next_mixer_weights, layer + 1, n // tm)
    w_up, w_down = weights
    out, *mixer_bf16 = pl.pallas_call(
        functools.partial(_mlp_kernel, final_norm=last),
        grid=(n // tm,),
        in_specs=[tok(D_MODEL), _resident((None, 1, D_MODEL), lambda i: (layer, 0, 0)),
                  const(D_MODEL, D_FF), const(D_FF, D_MODEL), const(1, D_MODEL)] + cast_in,
        out_specs=[tok(D_MODEL)] + cast_out,
        out_shape=[jax.ShapeDtypeStruct(x.shape, F32)] + cast_shapes,
        compiler_params=pltpu.CompilerParams(
            dimension_semantics=("arbitrary",), vmem_limit_bytes=VMEM_LIMIT_BYTES),
        name=f"mlp_l{layer}",
    )(x, norm_mlp, w_up, w_down, norm_final, *([] if last else next_mixer_weights))
    return out, mixer_bf16


def kernel(x, positions, norm_mix, w_in, conv_w, w_conv_out, ret_norm, w_ret_out, w_o, norm_mlp, w_up, w_down, norm_final):
    bsz, seq, _ = x.shape
    assert seq % TOKEN_TILE == 0 and TOKEN_TILE % CHUNK == 0
    mixer_f32 = (w_in, w_conv_out, w_ret_out, w_o)
    mlp_f32 = (w_up, w_down)
    cos_t, sin_t, mixer_bf16 = _rope_tables(positions, mixer_f32)
    decays = _decay_tables()
    as_row = lambda a: a[:, None, :]
    norm_mix, ret_norm, norm_mlp = as_row(norm_mix), as_row(ret_norm), as_row(norm_mlp)
    norm_final = norm_final[None, :]
    x = x.reshape(bsz * seq, D_MODEL)
    for layer in range(DEPTH):
        x, mlp_bf16 = _mixer(x, cos_t, sin_t, norm_mix, conv_w, ret_norm, mixer_bf16, decays,
                             mlp_f32, layer, seq)
        x, mixer_bf16 = _mlp(x, norm_mlp, mlp_bf16, norm_final, mixer_f32, layer)
    return x.reshape(bsz, seq, D_MODEL)
```

```python
import functools

import jax
import jax.numpy as jnp
import numpy as np
from jax import lax
from jax.experimental import pallas as pl
from jax.experimental.pallas import tpu as pltpu

D_MODEL = 1024
DEPTH = 2
D_CONV = D_MODEL
CONV_WIDTH = 3
RET_HEADS = 4
RET_QK_DIM = 128
RET_V_DIM = 256
RET_QK = RET_HEADS * RET_QK_DIM
RET_V = RET_HEADS * RET_V_DIM
CHUNK = 128
ROPE_BASE = 10000.0
D_FF = 4 * D_MODEL
EPS = 1e-6
D_IN = 3 * D_CONV + 2 * RET_QK + 2 * RET_V + 2 * D_MODEL

OFF_CB = 0
OFF_CC = OFF_CB + D_CONV
OFF_CU = OFF_CC + D_CONV
OFF_Q = OFF_CU + D_CONV
OFF_K = OFF_Q + RET_QK
OFF_V = OFF_K + RET_QK
OFF_G = OFF_V + RET_V
OFF_GA = OFF_G + RET_V
OFF_GB = OFF_GA + D_MODEL

V7X_SUBLANES = 8
BF16_SUBLANES = 2 * V7X_SUBLANES
TOKEN_TILE = 512
ROPE_TILE = 2048
MLP_TILE = 2 * TOKEN_TILE
MIXER_TILE = 2 * TOKEN_TILE
VMEM_LIMIT_BYTES = 58 * 1024 * 1024

F32 = jnp.float32
BF16 = jnp.bfloat16


def _rms(x, g):
    return x * lax.rsqrt(jnp.mean(x * x, axis=-1, keepdims=True) + EPS) * g


def _dot(a, b):
    return jnp.dot(a, b, preferred_element_type=F32)


def _sigmoid(x):
    return 0.5 * jnp.tanh(0.5 * x) + 0.5


N_CAST_MIXER = 4
N_CAST_MLP = 2


def _cast_rows(src_refs, dst_refs):
    for src, dst in zip(src_refs, dst_refs, strict=True):
        dst[...] = src[...].astype(BF16)


def _rope_kernel(pos_ref, inv_ref, *refs):
    cast_src, (cos_ref, sin_ref), cast_dst = (
        refs[:N_CAST_MIXER], refs[N_CAST_MIXER:N_CAST_MIXER + 2], refs[N_CAST_MIXER + 2:])
    _cast_rows(cast_src, cast_dst)
    ang = pos_ref[...].astype(F32) * inv_ref[...]
    cos = jnp.cos(ang)
    sin = jnp.sin(ang)
    cos_ref[...] = jnp.concatenate([cos, cos], axis=0).T
    sin_ref[...] = jnp.concatenate([-sin, sin], axis=0).T


def _rotate(t, cos_t, sin_t):
    return t * cos_t + pltpu.roll(t, RET_QK_DIM // 2, 1) * sin_t


def _mixer_kernel(x_ref, cos_ref, sin_ref, nm_ref, win_ref, cw_ref, wco_ref, rn_ref, wro_ref,
                  wo_ref, di_ref, dq_ref, dk_ref, dc_ref, *refs, tiles_per_seq):
    cast_src, o_ref, cast_dst, (state_ref, z_ref, xg_ref, h_ref, s_ref) = (
        refs[:N_CAST_MLP], refs[N_CAST_MLP], refs[N_CAST_MLP + 1:2 * N_CAST_MLP + 1],
        refs[2 * N_CAST_MLP + 1:])
    _cast_rows(cast_src, cast_dst)
    tm = TOKEN_TILE
    n_chunks = tm // CHUNK
    rows = lambda c: slice(c * CHUNK, (c + 1) * CHUNK)
    qk_cols = lambda hd: slice(hd * RET_QK_DIM, (hd + 1) * RET_QK_DIM)
    v_cols = lambda hd: slice(hd * RET_V_DIM, (hd + 1) * RET_V_DIM)
    blocks = [(hd, c) for hd in range(RET_HEADS) for c in range(n_chunks)]

    @pl.when(pl.program_id(0) % tiles_per_seq == 0)
    def _():
        state_ref[...] = jnp.zeros_like(state_ref)
        z_ref[...] = jnp.zeros_like(z_ref)

    def one_pass(r0):
        tile_rows = slice(r0, r0 + tm)
        x = x_ref[tile_rows, :]
        xg = x * nm_ref[...]
        xg_ref[...] = xg.astype(BF16)
        cc = _dot(xg_ref[...], win_ref[:, OFF_CC:OFF_CC + D_CONV])
        cu = _dot(xg_ref[...], win_ref[:, OFF_CU:OFF_CU + D_CONV])
        r = lax.rsqrt(jnp.mean(x * x, axis=-1, keepdims=True) + EPS)
        h_ref[...] = (xg * r).astype(BF16)

        def proj(off, size):
            return _dot(h_ref[...], win_ref[:, off:off + size])

        q = proj(OFF_Q, RET_QK)
        k = proj(OFF_K, RET_QK)
        z = cc * cu * (r * r)
        z_head = jnp.concatenate([z_ref[...], z[0:V7X_SUBLANES]], axis=0)
        z_ref[...] = z[tm - V7X_SUBLANES:tm]
        cw = cw_ref[...]

        def conv3(z0):
            return cw[0:1] * pltpu.roll(z0, 2, 0) + cw[1:2] * pltpu.roll(z0, 1, 0) + cw[2:3] * z0

        conv = jnp.concatenate([conv3(z_head)[V7X_SUBLANES:], conv3(z)[V7X_SUBLANES:]], axis=0)

        v = proj(OFF_V, RET_V).astype(BF16)
        cos_t = cos_ref[tile_rows, :]
        sin_t = sin_ref[tile_rows, :]

        def rotate_heads(t):
            return jnp.concatenate([_rotate(t[:, qk_cols(hd)], cos_t, sin_t) for hd in range(RET_HEADS)],
                                   axis=1)

        qr = rotate_heads(q) * (RET_QK_DIM ** -0.5)
        kr = rotate_heads(k)
        q_intra = qr.astype(BF16)
        q_inter = (qr * dq_ref[...]).astype(BF16)
        k_intra = kr.astype(BF16)
        k_state = (kr * dk_ref[...]).astype(BF16)
        g = proj(OFF_G, RET_V)

        scores = {}
        incr = {}
        for hd, c in blocks:
            scores[hd, c] = (lax.dot_general(q_intra[rows(c), qk_cols(hd)], k_intra[rows(c), qk_cols(hd)],
                                             (((1,), (1,)), ((), ())), preferred_element_type=F32)
                             * di_ref[hd]).astype(BF16)
            incr[hd, c] = lax.dot_general(k_state[rows(c), qk_cols(hd)], v[rows(c), v_cols(hd)],
                                          (((0,), (0,)), ((), ())), preferred_element_type=F32)

        s_conv = (proj(OFF_CB, D_CONV) * conv).astype(BF16)
        state_in = {}
        for hd in range(RET_HEADS):
            st = state_ref[hd]
            for c in range(n_chunks):
                state_in[hd, c] = st.astype(BF16)
                st = st * dc_ref[hd] + incr[hd, c]
            state_ref[hd] = st
        half_g = 0.5 * g
        gate = (half_g * rn_ref[...]) * (jnp.tanh(half_g) + 1.0)

        for hd, c in blocks:
            lhs = jnp.concatenate([scores[hd, c], q_inter[rows(c), qk_cols(hd)]], axis=1)
            rhs = jnp.concatenate([v[rows(c), v_cols(hd)], state_in[hd, c]], axis=0)
            o = _dot(lhs, rhs)
            mu = jnp.mean(o, axis=-1, keepdims=True)
            d = o - mu
            var = jnp.mean(d * d, axis=-1, keepdims=True)
            s_ref[rows(c), v_cols(hd)] = (d * lax.rsqrt(var + EPS) * gate[rows(c), v_cols(hd)]).astype(BF16)
        y_conv = _dot(s_conv, wco_ref[...])
        sig_a_conv = _sigmoid(proj(OFF_GA, D_MODEL)) * y_conv
        y_ret = _dot(s_ref[...], wro_ref[...])
        merged = sig_a_conv + _sigmoid(proj(OFF_GB, D_MODEL)) * y_ret
        o_ref[tile_rows, :] = x + _dot(merged.astype(BF16), wo_ref[...])

    for r0 in range(0, x_ref.shape[0], tm):
        one_pass(r0)


def _mlp_kernel(x_ref, nm_ref, wup_ref, wdn_ref, nf_ref, *refs, final_norm):
    n_cast = len(refs) // 2
    _cast_rows(refs[:n_cast], refs[n_cast + 1:])
    o_ref = refs[n_cast]
    for r0 in range(0, x_ref.shape[0], TOKEN_TILE):
        tile_rows = slice(r0, r0 + TOKEN_TILE)
        x = x_ref[tile_rows, :]
        u = _dot((x * nm_ref[...]).astype(BF16), wup_ref[...])
        a = jnp.square(jnp.maximum(u, 0.0)).astype(BF16)
        r = lax.rsqrt(jnp.mean(x * x, axis=-1, keepdims=True) + EPS)
        y = x + (r * r) * _dot(a, wdn_ref[...])
        if final_norm:
            y = _rms(y, nf_ref[...])
        o_ref[tile_rows, :] = y


def _resident(shape, index_map):
    return pl.BlockSpec(shape, index_map, pipeline_mode=pl.Buffered(1))


def _decay_tables():
    f32 = np.float32
    log_gamma = np.log1p(-np.exp2(f32(-5.0) - np.arange(RET_HEADS, dtype=f32)))
    pos = np.arange(CHUNK, dtype=f32)
    rel = pos[:, None] - pos[None, :]
    decay_intra = np.where(rel >= 0, np.exp(log_gamma[:, None, None] * np.maximum(rel, f32(0.0))),
                           f32(0.0))
    decay_q = np.exp((pos[:, None] + f32(1.0)) * log_gamma)
    decay_k = np.exp((f32(CHUNK - 1.0) - pos[:, None]) * log_gamma)
    decay_chunk = np.exp(f32(CHUNK) * log_gamma)
    tile_layout = lambda d: np.tile(np.repeat(d, RET_QK_DIM, axis=1), (TOKEN_TILE // CHUNK, 1))
    dc = np.broadcast_to(decay_chunk[:, None, None], (RET_HEADS, 1, RET_V_DIM))
    tables = (decay_intra, tile_layout(decay_q), tile_layout(decay_k), dc)
    assert all(t.dtype == f32 for t in tables)
    return tuple(jnp.asarray(t) for t in tables)


def _cast_job(weights, layer, n_steps):
    in_specs, out_specs, out_shapes = [], [], []
    for w in weights:
        _, n_rows, n_cols = w.shape
        block_rows = n_rows // n_steps
        assert block_rows * n_steps == n_rows and block_rows % BF16_SUBLANES == 0
        in_specs.append(pl.BlockSpec((None, block_rows, n_cols), lambda i: (layer, i, 0)))
        out_specs.append(pl.BlockSpec((block_rows, n_cols), lambda i: (i, 0)))
        out_shapes.append(jax.ShapeDtypeStruct((n_rows, n_cols), BF16))
    return in_specs, out_specs, out_shapes


def _rope_tables(positions, mixer_weights):
    bsz, seq = positions.shape
    half = RET_QK_DIM // 2
    inv_freq = (ROPE_BASE ** (-jnp.arange(half, dtype=F32) / half))[:, None]
    n = bsz * seq
    tm = ROPE_TILE
    assert n % tm == 0
    cast_in, cast_out, cast_shapes = _cast_job(mixer_weights, 0, n // tm)
    cos_t, sin_t, *weights = pl.pallas_call(
        _rope_kernel,
        grid=(n // tm,),
        in_specs=[pl.BlockSpec((None, 1, tm), lambda i: (i, 0, 0)),
                  pl.BlockSpec((half, 1), lambda i: (0, 0))] + cast_in,
        out_specs=[pl.BlockSpec((tm, RET_QK_DIM), lambda i: (i, 0))] * 2 + cast_out,
        out_shape=[jax.ShapeDtypeStruct((n, RET_QK_DIM), F32)] * 2 + cast_shapes,
        name="rope_tables",
    )(positions.reshape(n // tm, 1, tm), inv_freq, *mixer_weights)
    return cos_t, sin_t, weights


def _mixer(x, cos_t, sin_t, norm_mix, conv_w, ret_norm, weights, decays, mlp_weights, layer, seq):
    n = x.shape[0]
    tm = MIXER_TILE
    assert n % tm == 0 and seq % tm == 0
    tok = lambda width: pl.BlockSpec((tm, width), lambda i: (i, 0))
    lay = lambda *shape: _resident((None,) + shape, lambda i: (layer,) + (0,) * len(shape))
    const = lambda *shape: _resident(shape, lambda i: (0,) * len(shape))
    w_in, w_conv_out, w_ret_out, w_o = weights
    cast_in, cast_out, cast_shapes = _cast_job(mlp_weights, layer, n // tm)
    out, *mlp_bf16 = pl.pallas_call(
        functools.partial(_mixer_kernel, tiles_per_seq=seq // tm),
        grid=(n // tm,),
        in_specs=[tok(D_MODEL), tok(RET_QK_DIM), tok(RET_QK_DIM),
                  lay(1, D_MODEL), const(D_MODEL, D_IN), lay(CONV_WIDTH, D_CONV),
                  const(D_CONV, D_MODEL), lay(1, RET_V), const(RET_V, D_MODEL), const(D_MODEL, D_MODEL),
                  const(RET_HEADS, CHUNK, CHUNK), const(TOKEN_TILE, RET_QK), const(TOKEN_TILE, RET_QK),
                  const(RET_HEADS, 1, RET_V_DIM)] + cast_in,
        out_specs=[tok(D_MODEL)] + cast_out,
        out_shape=[jax.ShapeDtypeStruct(x.shape, F32)] + cast_shapes,
        scratch_shapes=[pltpu.VMEM((RET_HEADS, RET_QK_DIM, RET_V_DIM), F32),
                        pltpu.VMEM((V7X_SUBLANES, D_CONV), F32),
                        pltpu.VMEM((TOKEN_TILE, D_MODEL), BF16),
                        pltpu.VMEM((TOKEN_TILE, D_MODEL), BF16),
                        pltpu.VMEM((TOKEN_TILE, RET_V), BF16)],
        compiler_params=pltpu.CompilerParams(
            dimension_semantics=("arbitrary",), vmem_limit_bytes=VMEM_LIMIT_BYTES),
        name=f"mixer_l{layer}",
    )(x, cos_t, sin_t, norm_mix, w_in, conv_w, w_conv_out, ret_norm, w_ret_out, w_o, *decays,
      *mlp_weights)
    return out, mlp_bf16


def _mlp(x, norm_mlp, weights, norm_final, next_mixer_weights, layer):
    n = x.shape[0]
    tm = MLP_TILE
    assert n % tm == 0
    tok = lambda width: pl.BlockSpec((tm, width), lambda i: (i, 0))
    const = lambda *shape: _resident(shape, lambda i: (0,) * len(shape))
    last = layer == DEPTH - 1
    cast_in, cast_out, cast_shapes = ([], [], []) if last else _cast_job(
        next_mixer_weights, layer + 1, n // tm)
    w_up, w_down = weights
    out, *mixer_bf16 = pl.pallas_call(
        functools.partial(_mlp_kernel, final_norm=last),
        grid=(n // tm,),
        in_specs=[tok(D_MODEL), _resident((None, 1, D_MODEL), lambda i: (layer, 0, 0)),
                  const(D_MODEL, D_FF), const(D_FF, D_MODEL), const(1, D_MODEL)] + cast_in,
        out_specs=[tok(D_MODEL)] + cast_out,
        out_shape=[jax.ShapeDtypeStruct(x.shape, F32)] + cast_shapes,
        compiler_params=pltpu.CompilerParams(
            dimension_semantics=("arbitrary",), vmem_limit_bytes=VMEM_LIMIT_BYTES),
        name=f"mlp_l{layer}",
    )(x, norm_mlp, w_up, w_down, norm_final, *([] if last else next_mixer_weights))
    return out, mixer_bf16


def kernel(x, positions, norm_mix, w_in, conv_w, w_conv_out, ret_norm, w_ret_out, w_o, norm_mlp, w_up, w_down, norm_final):
    bsz, seq, _ = x.shape
    assert seq % TOKEN_TILE == 0 and TOKEN_TILE % CHUNK == 0
    mixer_f32 = (w_in, w_conv_out, w_ret_out, w_o)
    mlp_f32 = (w_up, w_down)
    cos_t, sin_t, mixer_bf16 = _rope_tables(positions, mixer_f32)
    decays = _decay_tables()
    as_row = lambda a: a[:, None, :]
    norm_mix, ret_norm, norm_mlp = as_row(norm_mix), as_row(ret_norm), as_row(norm_mlp)
    norm_final = norm_final[None, :]
    x = x.reshape(bsz * seq, D_MODEL)
    for layer in range(DEPTH):
        x, mlp_bf16 = _mixer(x, cos_t, sin_t, norm_mix, conv_w, ret_norm, mixer_bf16, decays,
                             mlp_f32, layer, seq)
        x, mixer_bf16 = _mlp(x, norm_mlp, mlp_bf16, norm_final, mixer_f32, layer)
    return x.reshape(bsz, seq, D_MODEL)
```

```python
import functools

import jax
import jax.numpy as jnp
import numpy as np
from jax import lax
from jax.experimental import pallas as pl
from jax.experimental.pallas import tpu as pltpu

D_MODEL = 1024
DEPTH = 2
D_CONV = D_MODEL
CONV_WIDTH = 3
RET_HEADS = 4
RET_QK_DIM = 128
RET_V_DIM = 256
RET_QK = RET_HEADS * RET_QK_DIM
RET_V = RET_HEADS * RET_V_DIM
CHUNK = 128
ROPE_BASE = 10000.0
D_FF = 4 * D_MODEL
EPS = 1e-6
D_IN = 3 * D_CONV + 2 * RET_QK + 2 * RET_V + 2 * D_MODEL

OFF_CB = 0
OFF_CC = OFF_CB + D_CONV
OFF_CU = OFF_CC + D_CONV
OFF_Q = OFF_CU + D_CONV
OFF_K = OFF_Q + RET_QK
OFF_V = OFF_K + RET_QK
OFF_G = OFF_V + RET_V
OFF_GA = OFF_G + RET_V
OFF_GB = OFF_GA + D_MODEL

V7X_SUBLANES = 8
BF16_SUBLANES = 2 * V7X_SUBLANES
TOKEN_TILE = 512
ROPE_TILE = 2048
MLP_TILE = 2 * TOKEN_TILE
MIXER_TILE = 2 * TOKEN_TILE
VMEM_LIMIT_BYTES = 58 * 1024 * 1024

F32 = jnp.float32
BF16 = jnp.bfloat16


def _rms(x, g):
    return x * lax.rsqrt(jnp.mean(x * x, axis=-1, keepdims=True) + EPS) * g


def _dot(a, b):
    return jnp.dot(a, b, preferred_element_type=F32)


def _twice_sigmoid_of_twice(x):
    return jnp.tanh(x) + 1.0


N_CAST_MIXER = 4
N_CAST_MLP = 2
MIXER_CAST_SCALE = ((OFF_G, 0.5), None, None, (0, 0.5))


def _cast_rows(src_refs, dst_refs, scales=None):
    for k, (src, dst) in enumerate(zip(src_refs, dst_refs, strict=True)):
        if scales is None or scales[k] is None:
            dst[...] = src[...].astype(BF16)
        else:
            col, factor = scales[k]
            if col:
                dst[:, :col] = src[:, :col].astype(BF16)
            dst[:, col:] = (src[:, col:] * factor).astype(BF16)


def _rope_kernel(pos_ref, inv_ref, *refs):
    cast_src, (cos_ref, sin_ref), cast_dst = (
        refs[:N_CAST_MIXER], refs[N_CAST_MIXER:N_CAST_MIXER + 2], refs[N_CAST_MIXER + 2:])
    _cast_rows(cast_src, cast_dst, MIXER_CAST_SCALE)
    ang = pos_ref[...].astype(F32) * inv_ref[...]
    cos = jnp.cos(ang)
    sin = jnp.sin(ang)
    cos_ref[...] = jnp.concatenate([cos, cos], axis=0).T
    sin_ref[...] = jnp.concatenate([-sin, sin], axis=0).T


def _rotate(t, cos_t, sin_t):
    return t * cos_t + pltpu.roll(t, RET_QK_DIM // 2, 1) * sin_t


def _mixer_kernel(x_ref, cos_ref, sin_ref, nm_ref, win_ref, cw_ref, wco_ref, rn_ref, wro_ref,
                  wo_ref, di_ref, dq_ref, dk_ref, dc_ref, *refs, tiles_per_seq):
    cast_src, o_ref, cast_dst, (state_ref, z_ref, xg_ref, h_ref, s_ref) = (
        refs[:N_CAST_MLP], refs[N_CAST_MLP], refs[N_CAST_MLP + 1:2 * N_CAST_MLP + 1],
        refs[2 * N_CAST_MLP + 1:])
    _cast_rows(cast_src, cast_dst)
    tm = TOKEN_TILE
    n_chunks = tm // CHUNK
    rows = lambda c: slice(c * CHUNK, (c + 1) * CHUNK)
    qk_cols = lambda hd: slice(hd * RET_QK_DIM, (hd + 1) * RET_QK_DIM)
    v_cols = lambda hd: slice(hd * RET_V_DIM, (hd + 1) * RET_V_DIM)
    blocks = [(hd, c) for hd in range(RET_HEADS) for c in range(n_chunks)]

    @pl.when(pl.program_id(0) % tiles_per_seq == 0)
    def _():
        state_ref[...] = jnp.zeros_like(state_ref)
        z_ref[...] = jnp.zeros_like(z_ref)

    def one_pass(r0):
        tile_rows = slice(r0, r0 + tm)
        x = x_ref[tile_rows, :]
        xg = x * nm_ref[...]
        xg_ref[...] = xg.astype(BF16)
        cc = _dot(xg_ref[...], win_ref[:, OFF_CC:OFF_CC + D_CONV])
        cu = _dot(xg_ref[...], win_ref[:, OFF_CU:OFF_CU + D_CONV])
        r = lax.rsqrt(jnp.mean(x * x, axis=-1, keepdims=True) + EPS)
        h_ref[...] = (xg * r).astype(BF16)

        def proj(off, size):
            return _dot(h_ref[...], win_ref[:, off:off + size])

        q = proj(OFF_Q, RET_QK)
        k = proj(OFF_K, RET_QK)
        z = cc * cu * (r * r)
        z_head = jnp.concatenate([z_ref[...], z[0:V7X_SUBLANES]], axis=0)
        z_ref[...] = z[tm - V7X_SUBLANES:tm]
        cw = cw_ref[...]

        def conv3(z0):
            return cw[0:1] * pltpu.roll(z0, 2, 0) + cw[1:2] * pltpu.roll(z0, 1, 0) + cw[2:3] * z0

        conv = jnp.concatenate([conv3(z_head)[V7X_SUBLANES:], conv3(z)[V7X_SUBLANES:]], axis=0)

        v = proj(OFF_V, RET_V).astype(BF16)
        cos_t = cos_ref[tile_rows, :]
        sin_t = sin_ref[tile_rows, :]

        def rotate_heads(t):
            return jnp.concatenate([_rotate(t[:, qk_cols(hd)], cos_t, sin_t) for hd in range(RET_HEADS)],
                                   axis=1)

        qr = rotate_heads(q)
        kr = rotate_heads(k)
        q_intra = qr.astype(BF16)
        q_inter = (qr * dq_ref[...]).astype(BF16)
        k_intra = kr.astype(BF16)
        k_state = (kr * dk_ref[...]).astype(BF16)
        g = proj(OFF_G, RET_V)

        scores = {}
        incr = {}
        for hd, c in blocks:
            scores[hd, c] = (lax.dot_general(q_intra[rows(c), qk_cols(hd)], k_intra[rows(c), qk_cols(hd)],
                                             (((1,), (1,)), ((), ())), preferred_element_type=F32)
                             * di_ref[hd]).astype(BF16)
            incr[hd, c] = lax.dot_general(k_state[rows(c), qk_cols(hd)], v[rows(c), v_cols(hd)],
                                          (((0,), (0,)), ((), ())), preferred_element_type=F32)

        s_conv = (proj(OFF_CB, D_CONV) * conv).astype(BF16)
        state_in = {}
        for hd in range(RET_HEADS):
            st = state_ref[hd]
            for c in range(n_chunks):
                state_in[hd, c] = st.astype(BF16)
                st = st * dc_ref[hd] + incr[hd, c]
            state_ref[hd] = st
        gate = (g * rn_ref[...]) * _twice_sigmoid_of_twice(g)

        for hd, c in blocks:
            lhs = jnp.concatenate([scores[hd, c], q_inter[rows(c), qk_cols(hd)]], axis=1)
            rhs = jnp.concatenate([v[rows(c), v_cols(hd)], state_in[hd, c]], axis=0)
            o = _dot(lhs, rhs)
            mu = jnp.mean(o, axis=-1, keepdims=True)
            d = o - mu
            var = jnp.mean(d * d, axis=-1, keepdims=True)
            s_ref[rows(c), v_cols(hd)] = (d * lax.rsqrt(var + EPS) * gate[rows(c), v_cols(hd)]).astype(BF16)
        y_conv = _dot(s_conv, wco_ref[...])
        sig_a_conv = _twice_sigmoid_of_twice(proj(OFF_GA, D_MODEL)) * y_conv
        y_ret = _dot(s_ref[...], wro_ref[...])
        merged = sig_a_conv + _twice_sigmoid_of_twice(proj(OFF_GB, D_MODEL)) * y_ret
        o_ref[tile_rows, :] = x + _dot(merged.astype(BF16), wo_ref[...])

    for r0 in range(0, x_ref.shape[0], tm):
        one_pass(r0)


def _mlp_kernel(x_ref, nm_ref, wup_ref, wdn_ref, nf_ref, *refs, final_norm):
    n_cast = len(refs) // 2
    _cast_rows(refs[:n_cast], refs[n_cast + 1:], MIXER_CAST_SCALE)
    o_ref = refs[n_cast]
    for r0 in range(0, x_ref.shape[0], TOKEN_TILE):
        tile_rows = slice(r0, r0 + TOKEN_TILE)
        x = x_ref[tile_rows, :]
        u = _dot((x * nm_ref[...]).astype(BF16), wup_ref[...])
        a = jnp.square(jnp.maximum(u, 0.0)).astype(BF16)
        r = lax.rsqrt(jnp.mean(x * x, axis=-1, keepdims=True) + EPS)
        y = x + (r * r) * _dot(a, wdn_ref[...])
        if final_norm:
            y = _rms(y, nf_ref[...])
        o_ref[tile_rows, :] = y


def _resident(shape, index_map):
    return pl.BlockSpec(shape, index_map, pipeline_mode=pl.Buffered(1))


def _decay_tables():
    f32 = np.float32
    log_gamma = np.log1p(-np.exp2(f32(-5.0) - np.arange(RET_HEADS, dtype=f32)))
    pos = np.arange(CHUNK, dtype=f32)
    rel = pos[:, None] - pos[None, :]
    decay_intra = np.where(rel >= 0, np.exp(log_gamma[:, None, None] * np.maximum(rel, f32(0.0))),
                           f32(0.0))
    decay_q = np.exp((pos[:, None] + f32(1.0)) * log_gamma)
    decay_k = np.exp((f32(CHUNK - 1.0) - pos[:, None]) * log_gamma)
    decay_chunk = np.exp(f32(CHUNK) * log_gamma)
    tile_layout = lambda d: np.tile(np.repeat(d, RET_QK_DIM, axis=1), (TOKEN_TILE // CHUNK, 1))
    dc = np.broadcast_to(decay_chunk[:, None, None], (RET_HEADS, 1, RET_V_DIM))
    scale = f32(RET_QK_DIM ** -0.5)
    tables = (decay_intra * scale, tile_layout(decay_q) * scale, tile_layout(decay_k), dc)
    assert all(t.dtype == f32 for t in tables)
    return tuple(jnp.asarray(t) for t in tables)


def _cast_job(weights, layer, n_steps):
    in_specs, out_specs, out_shapes = [], [], []
    for w in weights:
        _, n_rows, n_cols = w.shape
        block_rows = n_rows // n_steps
        assert block_rows * n_steps == n_rows and block_rows % BF16_SUBLANES == 0
        in_specs.append(pl.BlockSpec((None, block_rows, n_cols), lambda i: (layer, i, 0)))
        out_specs.append(pl.BlockSpec((block_rows, n_cols), lambda i: (i, 0)))
        out_shapes.append(jax.ShapeDtypeStruct((n_rows, n_cols), BF16))
    return in_specs, out_specs, out_shapes


def _rope_tables(positions, mixer_weights):
    bsz, seq = positions.shape
    half = RET_QK_DIM // 2
    inv_freq = (ROPE_BASE ** (-jnp.arange(half, dtype=F32) / half))[:, None]
    n = bsz * seq
    tm = ROPE_TILE
    assert n % tm == 0
    cast_in, cast_out, cast_shapes = _cast_job(mixer_weights, 0, n // tm)
    cos_t, sin_t, *weights = pl.pallas_call(
        _rope_kernel,
        grid=(n // tm,),
        in_specs=[pl.BlockSpec((None, 1, tm), lambda i: (i, 0, 0)),
                  pl.BlockSpec((half, 1), lambda i: (0, 0))] + cast_in,
        out_specs=[pl.BlockSpec((tm, RET_QK_DIM), lambda i: (i, 0))] * 2 + cast_out,
        out_shape=[jax.ShapeDtypeStruct((n, RET_QK_DIM), F32)] * 2 + cast_shapes,
        name="rope_tables",
    )(positions.reshape(n // tm, 1, tm), inv_freq, *mixer_weights)
    return cos_t, sin_t, weights


def _mixer(x, cos_t, sin_t, norm_mix, conv_w, ret_norm, weights, decays, mlp_weights, layer, seq):
    n = x.shape[0]
    tm = MIXER_TILE
    assert n % tm == 0 and seq % tm == 0
    tok = lambda width: pl.BlockSpec((tm, width), lambda i: (i, 0))
    lay = lambda *shape: _resident((None,) + shape, lambda i: (layer,) + (0,) * len(shape))
    const = lambda *shape: _resident(shape, lambda i: (0,) * len(shape))
    w_in, w_conv_out, w_ret_out, w_o = weights
    cast_in, cast_out, cast_shapes = _cast_job(mlp_weights, layer, n // tm)
    out, *mlp_bf16 = pl.pallas_call(
        functools.partial(_mixer_kernel, tiles_per_seq=seq // tm),
        grid=(n // tm,),
        in_specs=[tok(D_MODEL), tok(RET_QK_DIM), tok(RET_QK_DIM),
                  lay(1, D_MODEL), const(D_MODEL, D_IN), lay(CONV_WIDTH, D_CONV),
                  const(D_CONV, D_MODEL), lay(1, RET_V), const(RET_V, D_MODEL), const(D_MODEL, D_MODEL),
                  const(RET_HEADS, CHUNK, CHUNK), const(TOKEN_TILE, RET_QK), const(TOKEN_TILE, RET_QK),
                  const(RET_HEADS, 1, RET_V_DIM)] + cast_in,
        out_specs=[tok(D_MODEL)] + cast_out,
        out_shape=[jax.ShapeDtypeStruct(x.shape, F32)] + cast_shapes,
        scratch_shapes=[pltpu.VMEM((RET_HEADS, RET_QK_DIM, RET_V_DIM), F32),
                        pltpu.VMEM((V7X_SUBLANES, D_CONV), F32),
                        pltpu.VMEM((TOKEN_TILE, D_MODEL), BF16),
                        pltpu.VMEM((TOKEN_TILE, D_MODEL), BF16),
                        pltpu.VMEM((TOKEN_TILE, RET_V), BF16)],
        compiler_params=pltpu.CompilerParams(
            dimension_semantics=("arbitrary",), vmem_limit_bytes=VMEM_LIMIT_BYTES),
        name=f"mixer_l{layer}",
    )(x, cos_t, sin_t, norm_mix, w_in, conv_w, w_conv_out, ret_norm, w_ret_out, w_o, *decays,
      *mlp_weights)
    return out, mlp_bf16


def _mlp(x, norm_mlp, weights, norm_final, next_mixer_weights, layer):
    n = x.shape[0]
    tm = MLP_TILE
    assert n % tm == 0
    tok = lambda width: pl.BlockSpec((tm, width), lambda i: (i, 0))
    const = lambda *shape: _resident(shape, lambda i: (0,) * len(shape))
    last = layer == DEPTH - 1
    cast_in, cast_out, cast_shapes = ([], [], []) if last else _cast_job(
        next_mixer_weights, layer + 1, n // tm)
    w_up, w_down = weights
    out, *mixer_bf16 = pl.pallas_call(
        functools.partial(_mlp_kernel, final_norm=last),
        grid=(n // tm,),
        in_specs=[tok(D_MODEL), _resident((None, 1, D_MODEL), lambda i: (layer, 0, 0)),
                  const(D_MODEL, D_FF), const(D_FF, D_MODEL), const(1, D_MODEL)] + cast_in,
        out_specs=[tok(D_MODEL)] + cast_out,
        out_shape=[jax.ShapeDtypeStruct(x.shape, F32)] + cast_shapes,
        compiler_params=pltpu.CompilerParams(
            dimension_semantics=("arbitrary",), vmem_limit_bytes=VMEM_LIMIT_BYTES),
        name=f"mlp_l{layer}",
    )(x, norm_mlp, w_up, w_down, norm_final, *([] if last else next_mixer_weights))
    return out, mixer_bf16


def kernel(x, positions, norm_mix, w_in, conv_w, w_conv_out, ret_norm, w_ret_out, w_o, norm_mlp, w_up, w_down, norm_final):
    bsz, seq, _ = x.shape
    assert seq % TOKEN_TILE == 0 and TOKEN_TILE % CHUNK == 0
    mixer_f32 = (w_in, w_conv_out, w_ret_out, w_o)
    mlp_f32 = (w_up, w_down)
    cos_t, sin_t, mixer_bf16 = _rope_tables(positions, mixer_f32)
    decays = _decay_tables()
    as_row = lambda a: a[:, None, :]
    norm_mix, ret_norm, norm_mlp = as_row(norm_mix), as_row(ret_norm), as_row(norm_mlp)
    norm_final = norm_final[None, :]
    x = x.reshape(bsz * seq, D_MODEL)
    for layer in range(DEPTH):
        x, mlp_bf16 = _mixer(x, cos_t, sin_t, norm_mix, conv_w, ret_norm, mixer_bf16, decays,
                             mlp_f32, layer, seq)
        x, mixer_bf16 = _mlp(x, norm_mlp, mlp_bf16, norm_final, mixer_f32, layer)
    return x.reshape(bsz, seq, D_MODEL)
```

```python
import functools

import jax
import jax.numpy as jnp
import numpy as np
from jax import lax
from jax.experimental import pallas as pl
from jax.experimental.pallas import tpu as pltpu

D_MODEL = 1024
DEPTH = 2
D_CONV = D_MODEL
CONV_WIDTH = 3
RET_HEADS = 4
RET_QK_DIM = 128
RET_V_DIM = 256
RET_QK = RET_HEADS * RET_QK_DIM
RET_V = RET_HEADS * RET_V_DIM
CHUNK = 256
ROPE_BASE = 10000.0
D_FF = 4 * D_MODEL
EPS = 1e-6
D_IN = 3 * D_CONV + 2 * RET_QK + 2 * RET_V + 2 * D_MODEL

OFF_CB = 0
OFF_CC = OFF_CB + D_CONV
OFF_CU = OFF_CC + D_CONV
OFF_Q = OFF_CU + D_CONV
OFF_K = OFF_Q + RET_QK
OFF_V = OFF_K + RET_QK
OFF_G = OFF_V + RET_V
OFF_GA = OFF_G + RET_V
OFF_GB = OFF_GA + D_MODEL

V7X_SUBLANES = 8
BF16_SUBLANES = 2 * V7X_SUBLANES
TOKEN_TILE = 512
ROPE_TILE = 2048
MLP_TILE = 2 * TOKEN_TILE
MIXER_TILE = 2 * TOKEN_TILE
VMEM_LIMIT_BYTES = 58 * 1024 * 1024

F32 = jnp.float32
BF16 = jnp.bfloat16


def _rms(x, g):
    return x * lax.rsqrt(jnp.mean(x * x, axis=-1, keepdims=True) + EPS) * g


def _dot(a, b):
    return jnp.dot(a, b, preferred_element_type=F32)


def _sigmoid(x):
    return 0.5 * jnp.tanh(0.5 * x) + 0.5


N_CAST_MIXER = 4
N_CAST_MLP = 2


def _cast_rows(src_refs, dst_refs):
    for src, dst in zip(src_refs, dst_refs, strict=True):
        dst[...] = src[...].astype(BF16)


def _rope_kernel(pos_ref, inv_ref, *refs):
    cast_src, (cos_ref, sin_ref), cast_dst = (
        refs[:N_CAST_MIXER], refs[N_CAST_MIXER:N_CAST_MIXER + 2], refs[N_CAST_MIXER + 2:])
    _cast_rows(cast_src, cast_dst)
    ang = pos_ref[...].astype(F32) * inv_ref[...]
    cos = jnp.cos(ang)
    sin = jnp.sin(ang)
    cos_ref[...] = jnp.concatenate([cos, cos], axis=0).T
    sin_ref[...] = jnp.concatenate([-sin, sin], axis=0).T


def _rotate(t, cos_t, sin_t):
    return t * cos_t + pltpu.roll(t, RET_QK_DIM // 2, 1) * sin_t


def _mixer_kernel(x_ref, cos_ref, sin_ref, nm_ref, win_ref, cw_ref, wco_ref, rn_ref, wro_ref,
                  wo_ref, di_ref, dq_ref, dk_ref, dc_ref, *refs, tiles_per_seq):
    cast_src, o_ref, cast_dst, (state_ref, z_ref, xg_ref, h_ref, s_ref) = (
        refs[:N_CAST_MLP], refs[N_CAST_MLP], refs[N_CAST_MLP + 1:2 * N_CAST_MLP + 1],
        refs[2 * N_CAST_MLP + 1:])
    _cast_rows(cast_src, cast_dst)
    tm = TOKEN_TILE
    n_chunks = tm // CHUNK
    rows = lambda c: slice(c * CHUNK, (c + 1) * CHUNK)
    qk_cols = lambda hd: slice(hd * RET_QK_DIM, (hd + 1) * RET_QK_DIM)
    v_cols = lambda hd: slice(hd * RET_V_DIM, (hd + 1) * RET_V_DIM)
    blocks = [(hd, c) for hd in range(RET_HEADS) for c in range(n_chunks)]

    @pl.when(pl.program_id(0) % tiles_per_seq == 0)
    def _():
        state_ref[...] = jnp.zeros_like(state_ref)
        z_ref[...] = jnp.zeros_like(z_ref)

    def one_pass(r0):
        tile_rows = slice(r0, r0 + tm)
        x = x_ref[tile_rows, :]
        xg = x * nm_ref[...]
        xg_ref[...] = xg.astype(BF16)
        cc = _dot(xg_ref[...], win_ref[:, OFF_CC:OFF_CC + D_CONV])
        cu = _dot(xg_ref[...], win_ref[:, OFF_CU:OFF_CU + D_CONV])
        r = lax.rsqrt(jnp.mean(x * x, axis=-1, keepdims=True) + EPS)
        h_ref[...] = (xg * r).astype(BF16)

        def proj(off, size):
            return _dot(h_ref[...], win_ref[:, off:off + size])

        q = proj(OFF_Q, RET_QK)
        k = proj(OFF_K, RET_QK)
        z = cc * cu * (r * r)
        z_head = jnp.concatenate([z_ref[...], z[0:V7X_SUBLANES]], axis=0)
        z_ref[...] = z[tm - V7X_SUBLANES:tm]
        cw = cw_ref[...]

        def conv3(z0):
            return cw[0:1] * pltpu.roll(z0, 2, 0) + cw[1:2] * pltpu.roll(z0, 1, 0) + cw[2:3] * z0

        conv = jnp.concatenate([conv3(z_head)[V7X_SUBLANES:], conv3(z)[V7X_SUBLANES:]], axis=0)

        v = proj(OFF_V, RET_V).astype(BF16)
        cos_t = cos_ref[tile_rows, :]
        sin_t = sin_ref[tile_rows, :]

        def rotate_heads(t):
            return jnp.concatenate([_rotate(t[:, qk_cols(hd)], cos_t, sin_t) for hd in range(RET_HEADS)],
                                   axis=1)

        qr = rotate_heads(q) * (RET_QK_DIM ** -0.5)
        kr = rotate_heads(k)
        q_intra = qr.astype(BF16)
        q_inter = (qr * dq_ref[...]).astype(BF16)
        k_intra = kr.astype(BF16)
        k_state = (kr * dk_ref[...]).astype(BF16)
        g = proj(OFF_G, RET_V)

        scores = {}
        incr = {}
        for hd, c in blocks:
            scores[hd, c] = (lax.dot_general(q_intra[rows(c), qk_cols(hd)], k_intra[rows(c), qk_cols(hd)],
                                             (((1,), (1,)), ((), ())), preferred_element_type=F32)
                             * di_ref[hd]).astype(BF16)
            incr[hd, c] = lax.dot_general(k_state[rows(c), qk_cols(hd)], v[rows(c), v_cols(hd)],
                                          (((0,), (0,)), ((), ())), preferred_element_type=F32)

        s_conv = (proj(OFF_CB, D_CONV) * conv).astype(BF16)
        state_in = {}
        for hd in range(RET_HEADS):
            st = state_ref[hd]
            for c in range(n_chunks):
                state_in[hd, c] = st.astype(BF16)
                st = st * dc_ref[hd] + incr[hd, c]
            state_ref[hd] = st
        half_g = 0.5 * g
        gate = (half_g * rn_ref[...]) * (jnp.tanh(half_g) + 1.0)

        for hd, c in blocks:
            lhs = jnp.concatenate([scores[hd, c], q_inter[rows(c), qk_cols(hd)]], axis=1)
            rhs = jnp.concatenate([v[rows(c), v_cols(hd)], state_in[hd, c]], axis=0)
            o = _dot(lhs, rhs)
            mu = jnp.mean(o, axis=-1, keepdims=True)
            d = o - mu
            var = jnp.mean(d * d, axis=-1, keepdims=True)
            s_ref[rows(c), v_cols(hd)] = (d * lax.rsqrt(var + EPS) * gate[rows(c), v_cols(hd)]).astype(BF16)
        y_conv = _dot(s_conv, wco_ref[...])
        sig_a_conv = _sigmoid(proj(OFF_GA, D_MODEL)) * y_conv
        y_ret = _dot(s_ref[...], wro_ref[...])
        merged = sig_a_conv + _sigmoid(proj(OFF_GB, D_MODEL)) * y_ret
        o_ref[tile_rows, :] = x + _dot(merged.astype(BF16), wo_ref[...])

    for r0 in range(0, x_ref.shape[0], tm):
        one_pass(r0)


def _mlp_kernel(x_ref, nm_ref, wup_ref, wdn_ref, nf_ref, *refs, final_norm):
    n_cast = len(refs) // 2
    _cast_rows(refs[:n_cast], refs[n_cast + 1:])
    o_ref = refs[n_cast]
    for r0 in range(0, x_ref.shape[0], TOKEN_TILE):
        tile_rows = slice(r0, r0 + TOKEN_TILE)
        x = x_ref[tile_rows, :]
        u = _dot((x * nm_ref[...]).astype(BF16), wup_ref[...])
        a = jnp.square(jnp.maximum(u, 0.0)).astype(BF16)
        r = lax.rsqrt(jnp.mean(x * x, axis=-1, keepdims=True) + EPS)
        y = x + (r * r) * _dot(a, wdn_ref[...])
        if final_norm:
            y = _rms(y, nf_ref[...])
        o_ref[tile_rows, :] = y


def _resident(shape, index_map):
    return pl.BlockSpec(shape, index_map, pipeline_mode=pl.Buffered(1))


def _decay_tables():
    f32 = np.float32
    log_gamma = np.log1p(-np.exp2(f32(-5.0) - np.arange(RET_HEADS, dtype=f32)))
    pos = np.arange(CHUNK, dtype=f32)
    rel = pos[:, None] - pos[None, :]
    decay_intra = np.where(rel >= 0, np.exp(log_gamma[:, None, None] * np.maximum(rel, f32(0.0))),
                           f32(0.0))
    decay_q = np.exp((pos[:, None] + f32(1.0)) * log_gamma)
    decay_k = np.exp((f32(CHUNK - 1.0) - pos[:, None]) * log_gamma)
    decay_chunk = np.exp(f32(CHUNK) * log_gamma)
    tile_layout = lambda d: np.tile(np.repeat(d, RET_QK_DIM, axis=1), (TOKEN_TILE // CHUNK, 1))
    dc = np.broadcast_to(decay_chunk[:, None, None], (RET_HEADS, 1, RET_V_DIM))
    tables = (decay_intra, tile_layout(decay_q), tile_layout(decay_k), dc)
    assert all(t.dtype == f32 for t in tables)
    return tuple(jnp.asarray(t) for t in tables)


def _cast_job(weights, layer, n_steps):
    in_specs, out_specs, out_shapes = [], [], []
    for w in weights:
        _, n_rows, n_cols = w.shape
        block_rows = n_rows // n_steps
        assert block_rows * n_steps == n_rows and block_rows % BF16_SUBLANES == 0
        in_specs.append(pl.BlockSpec((None, block_rows, n_cols), lambda i: (layer, i, 0)))
        out_specs.append(pl.BlockSpec((block_rows, n_cols), lambda i: (i, 0)))
        out_shapes.append(jax.ShapeDtypeStruct((n_rows, n_cols), BF16))
    return in_specs, out_specs, out_shapes


def _rope_tables(positions, mixer_weights):
    bsz, seq = positions.shape
    half = RET_QK_DIM // 2
    inv_freq = (ROPE_BASE ** (-jnp.arange(half, dtype=F32) / half))[:, None]
    n = bsz * seq
    tm = ROPE_TILE
    assert n % tm == 0
    cast_in, cast_out, cast_shapes = _cast_job(mixer_weights, 0, n // tm)
    cos_t, sin_t, *weights = pl.pallas_call(
        _rope_kernel,
        grid=(n // tm,),
        in_specs=[pl.BlockSpec((None, 1, tm), lambda i: (i, 0, 0)),
                  pl.BlockSpec((half, 1), lambda i: (0, 0))] + cast_in,
        out_specs=[pl.BlockSpec((tm, RET_QK_DIM), lambda i: (i, 0))] * 2 + cast_out,
        out_shape=[jax.ShapeDtypeStruct((n, RET_QK_DIM), F32)] * 2 + cast_shapes,
        name="rope_tables",
    )(positions.reshape(n // tm, 1, tm), inv_freq, *mixer_weights)
    return cos_t, sin_t, weights


def _mixer(x, cos_t, sin_t, norm_mix, conv_w, ret_norm, weights, decays, mlp_weights, layer, seq):
    n = x.shape[0]
    tm = MIXER_TILE
    assert n % tm == 0 and seq % tm == 0
    tok = lambda width: pl.BlockSpec((tm, width), lambda i: (i, 0))
    lay = lambda *shape: _resident((None,) + shape, lambda i: (layer,) + (0,) * len(shape))
    const = lambda *shape: _resident(shape, lambda i: (0,) * len(shape))
    w_in, w_conv_out, w_ret_out, w_o = weights
    cast_in, cast_out, cast_shapes = _cast_job(mlp_weights, layer, n // tm)
    out, *mlp_bf16 = pl.pallas_call(
        functools.partial(_mixer_kernel, tiles_per_seq=seq // tm),
        grid=(n // tm,),
        in_specs=[tok(D_MODEL), tok(RET_QK_DIM), tok(RET_QK_DIM),
                  lay(1, D_MODEL), const(D_MODEL, D_IN), lay(CONV_WIDTH, D_CONV),
                  const(D_CONV, D_MODEL), lay(1, RET_V), const(RET_V, D_MODEL), const(D_MODEL, D_MODEL),
                  const(RET_HEADS, CHUNK, CHUNK), const(TOKEN_TILE, RET_QK), const(TOKEN_TILE, RET_QK),
                  const(RET_HEADS, 1, RET_V_DIM)] + cast_in,
        out_specs=[tok(D_MODEL)] + cast_out,
        out_shape=[jax.ShapeDtypeStruct(x.shape, F32)] + cast_shapes,
        scratch_shapes=[pltpu.VMEM((RET_HEADS, RET_QK_DIM, RET_V_DIM), F32),
                        pltpu.VMEM((V7X_SUBLANES, D_CONV), F32),
                        pltpu.VMEM((TOKEN_TILE, D_MODEL), BF16),
                        pltpu.VMEM((TOKEN_TILE, D_MODEL), BF16),
                        pltpu.VMEM((TOKEN_TILE, RET_V), BF16)],
        compiler_params=pltpu.CompilerParams(
            dimension_semantics=("arbitrary",), vmem_limit_bytes=VMEM_LIMIT_BYTES),
        name=f"mixer_l{layer}",
    )(x, cos_t, sin_t, norm_mix, w_in, conv_w, w_conv_out, ret_norm, w_ret_out, w_o, *decays,
      *mlp_weights)
    return out, mlp_bf16


def _mlp(x, norm_mlp, weights, norm_final, next_mixer_weights, layer):
    n = x.shape[0]
    tm = MLP_TILE
    assert n % tm == 0
    tok = lambda width: pl.BlockSpec((tm, width), lambda i: (i, 0))
    const = lambda *shape: _resident(shape, lambda i: (0,) * len(shape))
    last = layer == DEPTH - 1
    cast_in, cast_out, cast_shapes = ([], [], []) if last else _cast_job(
        next_mixer_weights, layer + 1, n // tm)
    w_up, w_down = weights
    out, *mixer_bf16 = pl.pallas_call(
        functools.partial(_mlp_kernel, final_norm=last),
        grid=(n // tm,),
        in_specs=[tok(D_MODEL), _resident((None, 1, D_MODEL), lambda i: (layer, 0, 0)),
                  const(D_MODEL, D_FF), const(D_FF, D_MODEL), const(1, D_MODEL)] + cast_in,
        out_specs=[tok(D_MODEL)] + cast_out,
        out_shape=[jax.ShapeDtypeStruct(x.shape, F32)] + cast_shapes,
        compiler_params=pltpu.CompilerParams(
            dimension_semantics=("arbitrary",), vmem_limit_bytes=VMEM_LIMIT_BYTES),
        name=f"mlp_l{layer}",
    )(x, norm_mlp, w_up, w_down, norm_final, *([] if last else next_mixer_weights))
    return out, mixer_bf16


def kernel(x, positions, norm_mix, w_in, conv_w, w_conv_out, ret_norm, w_ret_out, w_o, norm_mlp, w_up, w_down, norm_final):
    bsz, seq, _ = x.shape
    assert seq % TOKEN_TILE == 0 and TOKEN_TILE % CHUNK == 0
    mixer_f32 = (w_in, w_conv_out, w_ret_out, w_o)
    mlp_f32 = (w_up, w_down)
    cos_t, sin_t, mixer_bf16 = _rope_tables(positions, mixer_f32)
    decays = _decay_tables()
    as_row = lambda a: a[:, None, :]
    norm_mix, ret_norm, norm_mlp = as_row(norm_mix), as_row(ret_norm), as_row(norm_mlp)
    norm_final = norm_final[None, :]
    x = x.reshape(bsz * seq, D_MODEL)
    for layer in range(DEPTH):
        x, mlp_bf16 = _mixer(x, cos_t, sin_t, norm_mix, conv_w, ret_norm, mixer_bf16, decays,
                             mlp_f32, layer, seq)
        x, mixer_bf16 = _mlp(x, norm_mlp, mlp_bf16, norm_final, mixer_f32, layer)
    return x.reshape(bsz, seq, D_MODEL)
```

```python
import functools

import jax
import jax.numpy as jnp
import numpy as np
from jax import lax
from jax.experimental import pallas as pl
from jax.experimental.pallas import tpu as pltpu

D_MODEL = 1024
DEPTH = 2
D_CONV = D_MODEL
CONV_WIDTH = 3
RET_HEADS = 4
RET_QK_DIM = 128
RET_V_DIM = 256
RET_QK = RET_HEADS * RET_QK_DIM
RET_V = RET_HEADS * RET_V_DIM
CHUNK = 128
ROPE_BASE = 10000.0
D_FF = 4 * D_MODEL
EPS = 1e-6
D_IN = 3 * D_CONV + 2 * RET_QK + 2 * RET_V + 2 * D_MODEL

OFF_CB = 0
OFF_CC = OFF_CB + D_CONV
OFF_CU = OFF_CC + D_CONV
OFF_Q = OFF_CU + D_CONV
OFF_K = OFF_Q + RET_QK
OFF_V = OFF_K + RET_QK
OFF_G = OFF_V + RET_V
OFF_GA = OFF_G + RET_V
OFF_GB = OFF_GA + D_MODEL

V7X_SUBLANES = 8
BF16_SUBLANES = 2 * V7X_SUBLANES
TOKEN_TILE = 512
CAST_STEPS = 16
MLP_TILE = 2 * TOKEN_TILE
MIXER_TILE = 2 * TOKEN_TILE
VMEM_LIMIT_BYTES = 58 * 1024 * 1024

F32 = jnp.float32
BF16 = jnp.bfloat16


def _rms(x, g):
    return x * lax.rsqrt(jnp.mean(x * x, axis=-1, keepdims=True) + EPS) * g


def _dot(a, b):
    return jnp.dot(a, b, preferred_element_type=F32)


def _sigmoid(x):
    return 0.5 * jnp.tanh(0.5 * x) + 0.5


N_CAST_MLP = 2


def _cast_rows(src_refs, dst_refs):
    for src, dst in zip(src_refs, dst_refs, strict=True):
        dst[...] = src[...].astype(BF16)


def _cast_kernel(*refs):
    n_cast = len(refs) // 2
    _cast_rows(refs[:n_cast], refs[n_cast:])


def _rope_tables_tile(pos_ref, inv_ref):
    ang = pos_ref[...].astype(F32) * inv_ref[...]
    cos = jnp.cos(ang)
    sin = jnp.sin(ang)
    return jnp.concatenate([cos, cos], axis=0).T, jnp.concatenate([-sin, sin], axis=0).T


def _rotate(t, cos_t, sin_t):
    return t * cos_t + pltpu.roll(t, RET_QK_DIM // 2, 1) * sin_t


def _mixer_kernel(x_ref, rope_a_ref, rope_b_ref, nm_ref, win_ref, cw_ref, wco_ref, rn_ref, wro_ref,
                  wo_ref, di_ref, dq_ref, dk_ref, dc_ref, *refs, tiles_per_seq, make_rope):
    cast_src, refs = refs[:N_CAST_MLP], refs[N_CAST_MLP:]
    n_out = 1 + (2 if make_rope else 0)
    (o_ref, *rope_out), cast_dst, (state_ref, z_ref, xg_ref, h_ref, s_ref) = (
        refs[:n_out], refs[n_out:n_out + N_CAST_MLP], refs[n_out + N_CAST_MLP:])
    _cast_rows(cast_src, cast_dst)
    tm = TOKEN_TILE
    n_chunks = tm // CHUNK
    rows = lambda c: slice(c * CHUNK, (c + 1) * CHUNK)
    qk_cols = lambda hd: slice(hd * RET_QK_DIM, (hd + 1) * RET_QK_DIM)
    v_cols = lambda hd: slice(hd * RET_V_DIM, (hd + 1) * RET_V_DIM)
    blocks = [(hd, c) for hd in range(RET_HEADS) for c in range(n_chunks)]

    @pl.when(pl.program_id(0) % tiles_per_seq == 0)
    def _():
        state_ref[...] = jnp.zeros_like(state_ref)
        z_ref[...] = jnp.zeros_like(z_ref)

    def one_pass(r0):
        tile_rows = slice(r0, r0 + tm)
        x = x_ref[tile_rows, :]
        xg = x * nm_ref[...]
        xg_ref[...] = xg.astype(BF16)
        cc = _dot(xg_ref[...], win_ref[:, OFF_CC:OFF_CC + D_CONV])
        cu = _dot(xg_ref[...], win_ref[:, OFF_CU:OFF_CU + D_CONV])
        r = lax.rsqrt(jnp.mean(x * x, axis=-1, keepdims=True) + EPS)
        h_ref[...] = (xg * r).astype(BF16)

        def proj(off, size):
            return _dot(h_ref[...], win_ref[:, off:off + size])

        q = proj(OFF_Q, RET_QK)
        k = proj(OFF_K, RET_QK)
        z = cc * cu * (r * r)
        z_head = jnp.concatenate([z_ref[...], z[0:V7X_SUBLANES]], axis=0)
        z_ref[...] = z[tm - V7X_SUBLANES:tm]
        cw = cw_ref[...]

        def conv3(z0):
            return cw[0:1] * pltpu.roll(z0, 2, 0) + cw[1:2] * pltpu.roll(z0, 1, 0) + cw[2:3] * z0

        conv = jnp.concatenate([conv3(z_head)[V7X_SUBLANES:], conv3(z)[V7X_SUBLANES:]], axis=0)

        v = proj(OFF_V, RET_V).astype(BF16)
        if make_rope:
            cos_t, sin_t = _rope_tables_tile(rope_a_ref.at[r0 // tm], rope_b_ref)
            rope_out[0][tile_rows, :] = cos_t
            rope_out[1][tile_rows, :] = sin_t
        else:
            cos_t = rope_a_ref[tile_rows, :]
            sin_t = rope_b_ref[tile_rows, :]

        def rotate_heads(t):
            return jnp.concatenate([_rotate(t[:, qk_cols(hd)], cos_t, sin_t) for hd in range(RET_HEADS)],
                                   axis=1)

        qr = rotate_heads(q) * (RET_QK_DIM ** -0.5)
        kr = rotate_heads(k)
        q_intra = qr.astype(BF16)
        q_inter = (qr * dq_ref[...]).astype(BF16)
        k_intra = kr.astype(BF16)
        k_state = (kr * dk_ref[...]).astype(BF16)
        g = proj(OFF_G, RET_V)

        scores = {}
        incr = {}
        for hd, c in blocks:
            scores[hd, c] = (lax.dot_general(q_intra[rows(c), qk_cols(hd)], k_intra[rows(c), qk_cols(hd)],
                                             (((1,), (1,)), ((), ())), preferred_element_type=F32)
                             * di_ref[hd]).astype(BF16)
            incr[hd, c] = lax.dot_general(k_state[rows(c), qk_cols(hd)], v[rows(c), v_cols(hd)],
                                          (((0,), (0,)), ((), ())), preferred_element_type=F32)

        s_conv = (proj(OFF_CB, D_CONV) * conv).astype(BF16)
        state_in = {}
        for hd in range(RET_HEADS):
            st = state_ref[hd]
            for c in range(n_chunks):
                state_in[hd, c] = st.astype(BF16)
                st = st * dc_ref[hd] + incr[hd, c]
            state_ref[hd] = st
        half_g = 0.5 * g
        gate = (half_g * rn_ref[...]) * (jnp.tanh(half_g) + 1.0)

        for hd, c in blocks:
            lhs = jnp.concatenate([scores[hd, c], q_inter[rows(c), qk_cols(hd)]], axis=1)
            rhs = jnp.concatenate([v[rows(c), v_cols(hd)], state_in[hd, c]], axis=0)
            o = _dot(lhs, rhs)
            mu = jnp.mean(o, axis=-1, keepdims=True)
            d = o - mu
            var = jnp.mean(d * d, axis=-1, keepdims=True)
            s_ref[rows(c), v_cols(hd)] = (d * lax.rsqrt(var + EPS) * gate[rows(c), v_cols(hd)]).astype(BF16)
        y_conv = _dot(s_conv, wco_ref[...])
        sig_a_conv = _sigmoid(proj(OFF_GA, D_MODEL)) * y_conv
        y_ret = _dot(s_ref[...], wro_ref[...])
        merged = sig_a_conv + _sigmoid(proj(OFF_GB, D_MODEL)) * y_ret
        o_ref[tile_rows, :] = x + _dot(merged.astype(BF16), wo_ref[...])

    for r0 in range(0, x_ref.shape[0], tm):
        one_pass(r0)


def _mlp_kernel(x_ref, nm_ref, wup_ref, wdn_ref, nf_ref, *refs, final_norm):
    n_cast = len(refs) // 2
    _cast_rows(refs[:n_cast], refs[n_cast + 1:])
    o_ref = refs[n_cast]
    for r0 in range(0, x_ref.shape[0], TOKEN_TILE):
        tile_rows = slice(r0, r0 + TOKEN_TILE)
        x = x_ref[tile_rows, :]
        u = _dot((x * nm_ref[...]).astype(BF16), wup_ref[...])
        a = jnp.square(jnp.maximum(u, 0.0)).astype(BF16)
        r = lax.rsqrt(jnp.mean(x * x, axis=-1, keepdims=True) + EPS)
        y = x + (r * r) * _dot(a, wdn_ref[...])
        if final_norm:
            y = _rms(y, nf_ref[...])
        o_ref[tile_rows, :] = y


def _resident(shape, index_map):
    return pl.BlockSpec(shape, index_map, pipeline_mode=pl.Buffered(1))


def _decay_tables():
    f32 = np.float32
    log_gamma = np.log1p(-np.exp2(f32(-5.0) - np.arange(RET_HEADS, dtype=f32)))
    pos = np.arange(CHUNK, dtype=f32)
    rel = pos[:, None] - pos[None, :]
    decay_intra = np.where(rel >= 0, np.exp(log_gamma[:, None, None] * np.maximum(rel, f32(0.0))),
                           f32(0.0))
    decay_q = np.exp((pos[:, None] + f32(1.0)) * log_gamma)
    decay_k = np.exp((f32(CHUNK - 1.0) - pos[:, None]) * log_gamma)
    decay_chunk = np.exp(f32(CHUNK) * log_gamma)
    tile_layout = lambda d: np.tile(np.repeat(d, RET_QK_DIM, axis=1), (TOKEN_TILE // CHUNK, 1))
    dc = np.broadcast_to(decay_chunk[:, None, None], (RET_HEADS, 1, RET_V_DIM))
    tables = (decay_intra, tile_layout(decay_q), tile_layout(decay_k), dc)
    assert all(t.dtype == f32 for t in tables)
    return tuple(jnp.asarray(t) for t in tables)


def _cast_job(weights, layer, n_steps):
    in_specs, out_specs, out_shapes = [], [], []
    for w in weights:
        _, n_rows, n_cols = w.shape
        block_rows = n_rows // n_steps
        assert block_rows * n_steps == n_rows and block_rows % BF16_SUBLANES == 0
        in_specs.append(pl.BlockSpec((None, block_rows, n_cols), lambda i: (layer, i, 0)))
        out_specs.append(pl.BlockSpec((block_rows, n_cols), lambda i: (i, 0)))
        out_shapes.append(jax.ShapeDtypeStruct((n_rows, n_cols), BF16))
    return in_specs, out_specs, out_shapes


def _cast_weights(weights, layer):
    cast_in, cast_out, cast_shapes = _cast_job(weights, layer, CAST_STEPS)
    return pl.pallas_call(
        _cast_kernel, grid=(CAST_STEPS,), in_specs=cast_in, out_specs=cast_out,
        out_shape=cast_shapes, name="weight_cast")(*weights)


def _mixer(x, rope, norm_mix, conv_w, ret_norm, weights, decays, mlp_weights, layer, seq):
    n = x.shape[0]
    tm = MIXER_TILE
    assert n % tm == 0 and seq % tm == 0
    tok = lambda width: pl.BlockSpec((tm, width), lambda i: (i, 0))
    lay = lambda *shape: _resident((None,) + shape, lambda i: (layer,) + (0,) * len(shape))
    const = lambda *shape: _resident(shape, lambda i: (0,) * len(shape))
    w_in, w_conv_out, w_ret_out, w_o = weights
    cast_in, cast_out, cast_shapes = _cast_job(mlp_weights, layer, n // tm)
    make_rope = layer == 0
    passes = tm // TOKEN_TILE
    if make_rope:
        positions, inv_freq = rope
        rope = (positions.reshape(n // tm, passes, 1, TOKEN_TILE), inv_freq)
        rope_in = [pl.BlockSpec((None, passes, 1, TOKEN_TILE), lambda i: (i, 0, 0, 0)),
                   const(*inv_freq.shape)]
        rope_out = [tok(RET_QK_DIM)] * 2
        rope_shapes = [jax.ShapeDtypeStruct((n, RET_QK_DIM), F32)] * 2
    else:
        rope_in, rope_out, rope_shapes = [tok(RET_QK_DIM)] * 2, [], []
    out, *rest = pl.pallas_call(
        functools.partial(_mixer_kernel, tiles_per_seq=seq // tm, make_rope=make_rope),
        grid=(n // tm,),
        in_specs=[tok(D_MODEL)] + rope_in + [
            lay(1, D_MODEL), const(D_MODEL, D_IN), lay(CONV_WIDTH, D_CONV),
            const(D_CONV, D_MODEL), lay(1, RET_V), const(RET_V, D_MODEL), const(D_MODEL, D_MODEL),
            const(RET_HEADS, CHUNK, CHUNK), const(TOKEN_TILE, RET_QK), const(TOKEN_TILE, RET_QK),
            const(RET_HEADS, 1, RET_V_DIM)] + cast_in,
        out_specs=[tok(D_MODEL)] + rope_out + cast_out,
        out_shape=[jax.ShapeDtypeStruct(x.shape, F32)] + rope_shapes + cast_shapes,
        scratch_shapes=[pltpu.VMEM((RET_HEADS, RET_QK_DIM, RET_V_DIM), F32),
                        pltpu.VMEM((V7X_SUBLANES, D_CONV), F32),
                        pltpu.VMEM((TOKEN_TILE, D_MODEL), BF16),
                        pltpu.VMEM((TOKEN_TILE, D_MODEL), BF16),
                        pltpu.VMEM((TOKEN_TILE, RET_V), BF16)],
        compiler_params=pltpu.CompilerParams(
            dimension_semantics=("arbitrary",), vmem_limit_bytes=VMEM_LIMIT_BYTES),
        name=f"mixer_l{layer}",
    )(x, *rope, norm_mix, w_in, conv_w, w_conv_out, ret_norm, w_ret_out, w_o, *decays, *mlp_weights)
    tables = tuple(rest[:2]) if make_rope else rope
    return out, tables, rest[len(rope_out):]


def _mlp(x, norm_mlp, weights, norm_final, next_mixer_weights, layer):
    n = x.shape[0]
    tm = MLP_TILE
    assert n % tm == 0
    tok = lambda width: pl.BlockSpec((tm, width), lambda i: (i, 0))
    const = lambda *shape: _resident(shape, lambda i: (0,) * len(shape))
    last = layer == DEPTH - 1
    cast_in, cast_out, cast_shapes = ([], [], []) if last else _cast_job(
        next_mixer_weights, layer + 1, n // tm)
    w_up, w_down = weights
    out, *mixer_bf16 = pl.pallas_call(
        functools.partial(_mlp_kernel, final_norm=last),
        grid=(n // tm,),
        in_specs=[tok(D_MODEL), _resident((None, 1, D_MODEL), lambda i: (layer, 0, 0)),
                  const(D_MODEL, D_FF), const(D_FF, D_MODEL), const(1, D_MODEL)] + cast_in,
        out_specs=[tok(D_MODEL)] + cast_out,
        out_shape=[jax.ShapeDtypeStruct(x.shape, F32)] + cast_shapes,
        compiler_params=pltpu.CompilerParams(
            dimension_semantics=("arbitrary",), vmem_limit_bytes=VMEM_LIMIT_BYTES),
        name=f"mlp_l{layer}",
    )(x, norm_mlp, w_up, w_down, norm_final, *([] if last else next_mixer_weights))
    return out, mixer_bf16


def kernel(x, positions, norm_mix, w_in, conv_w, w_conv_out, ret_norm, w_ret_out, w_o, norm_mlp, w_up, w_down, norm_final):
    bsz, seq, _ = x.shape
    assert seq % TOKEN_TILE == 0 and TOKEN_TILE % CHUNK == 0
    mixer_f32 = (w_in, w_conv_out, w_ret_out, w_o)
    mlp_f32 = (w_up, w_down)
    mixer_bf16 = _cast_weights(mixer_f32, 0)
    half = RET_QK_DIM // 2
    rope = (positions, (ROPE_BASE ** (-jnp.arange(half, dtype=F32) / half))[:, None])
    decays = _decay_tables()
    as_row = lambda a: a[:, None, :]
    norm_mix, ret_norm, norm_mlp = as_row(norm_mix), as_row(ret_norm), as_row(norm_mlp)
    norm_final = norm_final[None, :]
    x = x.reshape(bsz * seq, D_MODEL)
    for layer in range(DEPTH):
        x, rope, mlp_bf16 = _mixer(x, rope, norm_mix, conv_w, ret_norm, mixer_bf16, decays, mlp_f32,
                                   layer, seq)
        x, mixer_bf16 = _mlp(x, norm_mlp, mlp_bf16, norm_final, mixer_f32, layer)
    return x.reshape(bsz, seq, D_MODEL)
```

```python
import functools

import jax
import jax.numpy as jnp
import numpy as np
from jax import lax
from jax.experimental import pallas as pl
from jax.experimental.pallas import tpu as pltpu

D_MODEL = 1024
DEPTH = 2
D_CONV = D_MODEL
CONV_WIDTH = 3
RET_HEADS = 4
RET_QK_DIM = 128
RET_V_DIM = 256
RET_QK = RET_HEADS * RET_QK_DIM
RET_V = RET_HEADS * RET_V_DIM
CHUNK = 128
ROPE_BASE = 10000.0
D_FF = 4 * D_MODEL
EPS = 1e-6
D_IN = 3 * D_CONV + 2 * RET_QK + 2 * RET_V + 2 * D_MODEL

OFF_CB = 0
OFF_CC = OFF_CB + D_CONV
OFF_CU = OFF_CC + D_CONV
OFF_Q = OFF_CU + D_CONV
OFF_K = OFF_Q + RET_QK
OFF_V = OFF_K + RET_QK
OFF_G = OFF_V + RET_V
OFF_GA = OFF_G + RET_V
OFF_GB = OFF_GA + D_MODEL

V7X_SUBLANES = 8
BF16_SUBLANES = 2 * V7X_SUBLANES
TOKEN_TILE = 512
CONV_BLOCK = 256
ROPE_TILE = 2048
MLP_TILE = 2 * TOKEN_TILE
MIXER_TILE = 2 * TOKEN_TILE
VMEM_LIMIT_BYTES = 58 * 1024 * 1024

F32 = jnp.float32
BF16 = jnp.bfloat16


def _rms(x, g):
    return x * lax.rsqrt(jnp.mean(x * x, axis=-1, keepdims=True) + EPS) * g


def _dot(a, b):
    return jnp.dot(a, b, preferred_element_type=F32)


def _sigmoid(x):
    return 0.5 * jnp.tanh(0.5 * x) + 0.5


N_CAST_MIXER = 4
N_CAST_MLP = 2


def _cast_rows(src_refs, dst_refs):
    for src, dst in zip(src_refs, dst_refs, strict=True):
        dst[...] = src[...].astype(BF16)


def _rope_kernel(pos_ref, inv_ref, *refs):
    cast_src, (cos_ref, sin_ref), cast_dst = (
        refs[:N_CAST_MIXER], refs[N_CAST_MIXER:N_CAST_MIXER + 2], refs[N_CAST_MIXER + 2:])
    _cast_rows(cast_src, cast_dst)
    ang = pos_ref[...].astype(F32) * inv_ref[...]
    cos = jnp.cos(ang)
    sin = jnp.sin(ang)
    cos_ref[...] = jnp.concatenate([cos, cos], axis=0).T
    sin_ref[...] = jnp.concatenate([-sin, sin], axis=0).T


def _rotate(t, cos_t, sin_t):
    return t * cos_t + pltpu.roll(t, RET_QK_DIM // 2, 1) * sin_t


def _mixer_kernel(x_ref, cos_ref, sin_ref, nm_ref, win_ref, cw_ref, wco_ref, rn_ref, wro_ref,
                  wo_ref, di_ref, dq_ref, dk_ref, dc_ref, *refs, tiles_per_seq):
    cast_src, o_ref, cast_dst, (state_ref, z_ref, xg_ref, h_ref, s_ref) = (
        refs[:N_CAST_MLP], refs[N_CAST_MLP], refs[N_CAST_MLP + 1:2 * N_CAST_MLP + 1],
        refs[2 * N_CAST_MLP + 1:])
    _cast_rows(cast_src, cast_dst)
    tm = TOKEN_TILE
    n_chunks = tm // CHUNK
    rows = lambda c: slice(c * CHUNK, (c + 1) * CHUNK)
    qk_cols = lambda hd: slice(hd * RET_QK_DIM, (hd + 1) * RET_QK_DIM)
    v_cols = lambda hd: slice(hd * RET_V_DIM, (hd + 1) * RET_V_DIM)
    blocks = [(hd, c) for hd in range(RET_HEADS) for c in range(n_chunks)]

    @pl.when(pl.program_id(0) % tiles_per_seq == 0)
    def _():
        state_ref[...] = jnp.zeros_like(state_ref)
        z_ref[...] = jnp.zeros_like(z_ref)

    def one_pass(r0):
        tile_rows = slice(r0, r0 + tm)
        x = x_ref[tile_rows, :]
        xg = x * nm_ref[...]
        xg_ref[...] = xg.astype(BF16)
        r = lax.rsqrt(jnp.mean(x * x, axis=-1, keepdims=True) + EPS)
        z = jnp.concatenate(
            [_dot(xg_ref[...], win_ref[:, OFF_CC + j:OFF_CC + j + CONV_BLOCK])
             * _dot(xg_ref[...], win_ref[:, OFF_CU + j:OFF_CU + j + CONV_BLOCK]) * (r * r)
             for j in range(0, D_CONV, CONV_BLOCK)], axis=1)
        h_ref[...] = (xg * r).astype(BF16)

        def proj(off, size):
            return _dot(h_ref[...], win_ref[:, off:off + size])

        q = proj(OFF_Q, RET_QK)
        k = proj(OFF_K, RET_QK)
        z_head = jnp.concatenate([z_ref[...], z[0:V7X_SUBLANES]], axis=0)
        z_ref[...] = z[tm - V7X_SUBLANES:tm]
        cw = cw_ref[...]

        def conv3(z0):
            return cw[0:1] * pltpu.roll(z0, 2, 0) + cw[1:2] * pltpu.roll(z0, 1, 0) + cw[2:3] * z0

        conv = jnp.concatenate([conv3(z_head)[V7X_SUBLANES:], conv3(z)[V7X_SUBLANES:]], axis=0)

        v = proj(OFF_V, RET_V).astype(BF16)
        cos_t = cos_ref[tile_rows, :]
        sin_t = sin_ref[tile_rows, :]

        def rotate_heads(t):
            return jnp.concatenate([_rotate(t[:, qk_cols(hd)], cos_t, sin_t) for hd in range(RET_HEADS)],
                                   axis=1)

        qr = rotate_heads(q) * (RET_QK_DIM ** -0.5)
        kr = rotate_heads(k)
        q_intra = qr.astype(BF16)
        q_inter = (qr * dq_ref[...]).astype(BF16)
        k_intra = kr.astype(BF16)
        k_state = (kr * dk_ref[...]).astype(BF16)
        g = proj(OFF_G, RET_V)

        scores = {}
        incr = {}
        for hd, c in blocks:
            scores[hd, c] = (lax.dot_general(q_intra[rows(c), qk_cols(hd)], k_intra[rows(c), qk_cols(hd)],
                                             (((1,), (1,)), ((), ())), preferred_element_type=F32)
                             * di_ref[hd]).astype(BF16)
            incr[hd, c] = lax.dot_general(k_state[rows(c), qk_cols(hd)], v[rows(c), v_cols(hd)],
                                          (((0,), (0,)), ((), ())), preferred_element_type=F32)

        s_conv = (proj(OFF_CB, D_CONV) * conv).astype(BF16)
        state_in = {}
        for hd in range(RET_HEADS):
            st = state_ref[hd]
            for c in range(n_chunks):
                state_in[hd, c] = st.astype(BF16)
                st = st * dc_ref[hd] + incr[hd, c]
            state_ref[hd] = st
        half_g = 0.5 * g
        gate = (half_g * rn_ref[...]) * (jnp.tanh(half_g) + 1.0)

        for hd, c in blocks:
            lhs = jnp.concatenate([scores[hd, c], q_inter[rows(c), qk_cols(hd)]], axis=1)
            rhs = jnp.concatenate([v[rows(c), v_cols(hd)], state_in[hd, c]], axis=0)
            o = _dot(lhs, rhs)
            mu = jnp.mean(o, axis=-1, keepdims=True)
            d = o - mu
            var = jnp.mean(d * d, axis=-1, keepdims=True)
            s_ref[rows(c), v_cols(hd)] = (d * lax.rsqrt(var + EPS) * gate[rows(c), v_cols(hd)]).astype(BF16)
        y_conv = _dot(s_conv, wco_ref[...])
        sig_a_conv = _sigmoid(proj(OFF_GA, D_MODEL)) * y_conv
        y_ret = _dot(s_ref[...], wro_ref[...])
        merged = sig_a_conv + _sigmoid(proj(OFF_GB, D_MODEL)) * y_ret
        o_ref[tile_rows, :] = x + _dot(merged.astype(BF16), wo_ref[...])

    for r0 in range(0, x_ref.shape[0], tm):
        one_pass(r0)


def _mlp_kernel(x_ref, nm_ref, wup_ref, wdn_ref, nf_ref, *refs, final_norm):
    n_cast = len(refs) // 2
    _cast_rows(refs[:n_cast], refs[n_cast + 1:])
    o_ref = refs[n_cast]
    for r0 in range(0, x_ref.shape[0], TOKEN_TILE):
        tile_rows = slice(r0, r0 + TOKEN_TILE)
        x = x_ref[tile_rows, :]
        u = _dot((x * nm_ref[...]).astype(BF16), wup_ref[...])
        a = jnp.square(jnp.maximum(u, 0.0)).astype(BF16)
        r = lax.rsqrt(jnp.mean(x * x, axis=-1, keepdims=True) + EPS)
        y = x + (r * r) * _dot(a, wdn_ref[...])
        if final_norm:
            y = _rms(y, nf_ref[...])
        o_ref[tile_rows, :] = y


def _resident(shape, index_map):
    return pl.BlockSpec(shape, index_map, pipeline_mode=pl.Buffered(1))


def _decay_tables():
    f32 = np.float32
    log_gamma = np.log1p(-np.exp2(f32(-5.0) - np.arange(RET_HEADS, dtype=f32)))
    pos = np.arange(CHUNK, dtype=f32)
    rel = pos[:, None] - pos[None, :]
    decay_intra = np.where(rel >= 0, np.exp(log_gamma[:, None, None] * np.maximum(rel, f32(0.0))),
                           f32(0.0))
    decay_q = np.exp((pos[:, None] + f32(1.0)) * log_gamma)
    decay_k = np.exp((f32(CHUNK - 1.0) - pos[:, None]) * log_gamma)
    decay_chunk = np.exp(f32(CHUNK) * log_gamma)
    tile_layout = lambda d: np.tile(np.repeat(d, RET_QK_DIM, axis=1), (TOKEN_TILE // CHUNK, 1))
    dc = np.broadcast_to(decay_chunk[:, None, None], (RET_HEADS, 1, RET_V_DIM))
    tables = (decay_intra, tile_layout(decay_q), tile_layout(decay_k), dc)
    assert all(t.dtype == f32 for t in tables)
    return tuple(jnp.asarray(t) for t in tables)


def _cast_job(weights, layer, n_steps):
    in_specs, out_specs, out_shapes = [], [], []
    for w in weights:
        _, n_rows, n_cols = w.shape
        block_rows = n_rows // n_steps
        assert block_rows * n_steps == n_rows and block_rows % BF16_SUBLANES == 0
        in_specs.append(pl.BlockSpec((None, block_rows, n_cols), lambda i: (layer, i, 0)))
        out_specs.append(pl.BlockSpec((block_rows, n_cols), lambda i: (i, 0)))
        out_shapes.append(jax.ShapeDtypeStruct((n_rows, n_cols), BF16))
    return in_specs, out_specs, out_shapes


def _rope_tables(positions, mixer_weights):
    bsz, seq = positions.shape
    half = RET_QK_DIM // 2
    inv_freq = (ROPE_BASE ** (-jnp.arange(half, dtype=F32) / half))[:, None]
    n = bsz * seq
    tm = ROPE_TILE
    assert n % tm == 0
    cast_in, cast_out, cast_shapes = _cast_job(mixer_weights, 0, n // tm)
    cos_t, sin_t, *weights = pl.pallas_call(
        _rope_kernel,
        grid=(n // tm,),
        in_specs=[pl.BlockSpec((None, 1, tm), lambda i: (i, 0, 0)),
                  pl.BlockSpec((half, 1), lambda i: (0, 0))] + cast_in,
        out_specs=[pl.BlockSpec((tm, RET_QK_DIM), lambda i: (i, 0))] * 2 + cast_out,
        out_shape=[jax.ShapeDtypeStruct((n, RET_QK_DIM), F32)] * 2 + cast_shapes,
        name="rope_tables",
    )(positions.reshape(n // tm, 1, tm), inv_freq, *mixer_weights)
    return cos_t, sin_t, weights


def _mixer(x, cos_t, sin_t, norm_mix, conv_w, ret_norm, weights, decays, mlp_weights, layer, seq):
    n = x.shape[0]
    tm = MIXER_TILE
    assert n % tm == 0 and seq % tm == 0
    tok = lambda width: pl.BlockSpec((tm, width), lambda i: (i, 0))
    lay = lambda *shape: _resident((None,) + shape, lambda i: (layer,) + (0,) * len(shape))
    const = lambda *shape: _resident(shape, lambda i: (0,) * len(shape))
    w_in, w_conv_out, w_ret_out, w_o = weights
    cast_in, cast_out, cast_shapes = _cast_job(mlp_weights, layer, n // tm)
    out, *mlp_bf16 = pl.pallas_call(
        functools.partial(_mixer_kernel, tiles_per_seq=seq // tm),
        grid=(n // tm,),
        in_specs=[tok(D_MODEL), tok(RET_QK_DIM), tok(RET_QK_DIM),
                  lay(1, D_MODEL), const(D_MODEL, D_IN), lay(CONV_WIDTH, D_CONV),
                  const(D_CONV, D_MODEL), lay(1, RET_V), const(RET_V, D_MODEL), const(D_MODEL, D_MODEL),
                  const(RET_HEADS, CHUNK, CHUNK), const(TOKEN_TILE, RET_QK), const(TOKEN_TILE, RET_QK),
                  const(RET_HEADS, 1, RET_V_DIM)] + cast_in,
        out_specs=[tok(D_MODEL)] + cast_out,
        out_shape=[jax.ShapeDtypeStruct(x.shape, F32)] + cast_shapes,
        scratch_shapes=[pltpu.VMEM((RET_HEADS, RET_QK_DIM, RET_V_DIM), F32),
                        pltpu.VMEM((V7X_SUBLANES, D_CONV), F32),
                        pltpu.VMEM((TOKEN_TILE, D_MODEL), BF16),
                        pltpu.VMEM((TOKEN_TILE, D_MODEL), BF16),
                        pltpu.VMEM((TOKEN_TILE, RET_V), BF16)],
        compiler_params=pltpu.CompilerParams(
            dimension_semantics=("arbitrary",), vmem_limit_bytes=VMEM_LIMIT_BYTES),
        name=f"mixer_l{layer}",
    )(x, cos_t, sin_t, norm_mix, w_in, conv_w, w_conv_out, ret_norm, w_ret_out, w_o, *decays,
      *mlp_weights)
    return out, mlp_bf16


def _mlp(x, norm_mlp, weights, norm_final, next_mixer_weights, layer):
    n = x.shape[0]
    tm = MLP_TILE
    assert n % tm == 0
    tok = lambda width: pl.BlockSpec((tm, width), lambda i: (i, 0))
    const = lambda *shape: _resident(shape, lambda i: (0,) * len(shape))
    last = layer == DEPTH - 1
    cast_in, cast_out, cast_shapes = ([], [], []) if last else _cast_job(
        next_mixer_weights, layer + 1, n // tm)
    w_up, w_down = weights
    out, *mixer_bf16 = pl.pallas_call(
        functools.partial(_mlp_kernel, final_norm=last),
        grid=(n // tm,),
        in_specs=[tok(D_MODEL), _resident((None, 1, D_MODEL), lambda i: (layer, 0, 0)),
                  const(D_MODEL, D_FF), const(D_FF, D_MODEL), const(1, D_MODEL)] + cast_in,
        out_specs=[tok(D_MODEL)] + cast_out,
        out_shape=[jax.ShapeDtypeStruct(x.shape, F32)] + cast_shapes,
        compiler_params=pltpu.CompilerParams(
            dimension_semantics=("arbitrary",), vmem_limit_bytes=VMEM_LIMIT_BYTES),
        name=f"mlp_l{layer}",
    )(x, norm_mlp, w_up, w_down, norm_final, *([] if last else next_mixer_weights))
    return out, mixer_bf16


def kernel(x, positions, norm_mix, w_in, conv_w, w_conv_out, ret_norm, w_ret_out, w_o, norm_mlp, w_up, w_down, norm_final):
    bsz, seq, _ = x.shape
    assert seq % TOKEN_TILE == 0 and TOKEN_TILE % CHUNK == 0
    mixer_f32 = (w_in, w_conv_out, w_ret_out, w_o)
    mlp_f32 = (w_up, w_down)
    cos_t, sin_t, mixer_bf16 = _rope_tables(positions, mixer_f32)
    decays = _decay_tables()
    as_row = lambda a: a[:, None, :]
    norm_mix, ret_norm, norm_mlp = as_row(norm_mix), as_row(ret_norm), as_row(norm_mlp)
    norm_final = norm_final[None, :]
    x = x.reshape(bsz * seq, D_MODEL)
    for layer in range(DEPTH):
        x, mlp_bf16 = _mixer(x, cos_t, sin_t, norm_mix, conv_w, ret_norm, mixer_bf16, decays,
                             mlp_f32, layer, seq)
        x, mixer_bf16 = _mlp(x, norm_mlp, mlp_bf16, norm_final, mixer_f32, layer)
    return x.reshape(bsz, seq, D_MODEL)
```

```python
import functools

import jax
import jax.numpy as jnp
import numpy as np
from jax import lax
from jax.experimental import pallas as pl
from jax.experimental.pallas import tpu as pltpu

D_MODEL = 1024
DEPTH = 2
D_CONV = D_MODEL
CONV_WIDTH = 3
RET_HEADS = 4
RET_QK_DIM = 128
RET_V_DIM = 256
RET_QK = RET_HEADS * RET_QK_DIM
RET_V = RET_HEADS * RET_V_DIM
CHUNK = 128
ROPE_BASE = 10000.0
D_FF = 4 * D_MODEL
EPS = 1e-6
D_IN = 3 * D_CONV + 2 * RET_QK + 2 * RET_V + 2 * D_MODEL

OFF_CB = 0
OFF_CC = OFF_CB + D_CONV
OFF_CU = OFF_CC + D_CONV
OFF_Q = OFF_CU + D_CONV
OFF_K = OFF_Q + RET_QK
OFF_V = OFF_K + RET_QK
OFF_G = OFF_V + RET_V
OFF_GA = OFF_G + RET_V
OFF_GB = OFF_GA + D_MODEL

V7X_SUBLANES = 8
BF16_SUBLANES = 2 * V7X_SUBLANES
TOKEN_TILE = 512
CONV_BLOCK = 256
FF_BLOCK = 1024
ROPE_TILE = 2048
MLP_TILE = 2 * TOKEN_TILE
MIXER_TILE = 2 * TOKEN_TILE
VMEM_LIMIT_BYTES = 58 * 1024 * 1024

F32 = jnp.float32
BF16 = jnp.bfloat16


def _rms(x, g):
    return x * lax.rsqrt(jnp.mean(x * x, axis=-1, keepdims=True) + EPS) * g


def _dot(a, b):
    return jnp.dot(a, b, preferred_element_type=F32)


def _sigmoid(x):
    return 0.5 * jnp.tanh(0.5 * x) + 0.5


N_CAST_MIXER = 4
N_CAST_MLP = 2


def _cast_rows(src_refs, dst_refs):
    for src, dst in zip(src_refs, dst_refs, strict=True):
        dst[...] = src[...].astype(BF16)


def _rope_kernel(pos_ref, inv_ref, *refs):
    cast_src, (cos_ref, sin_ref), cast_dst = (
        refs[:N_CAST_MIXER], refs[N_CAST_MIXER:N_CAST_MIXER + 2], refs[N_CAST_MIXER + 2:])
    _cast_rows(cast_src, cast_dst)
    ang = pos_ref[...].astype(F32) * inv_ref[...]
    cos = jnp.cos(ang)
    sin = jnp.sin(ang)
    cos_ref[...] = jnp.concatenate([cos, cos], axis=0).T
    sin_ref[...] = jnp.concatenate([-sin, sin], axis=0).T


def _rotate(t, cos_t, sin_t):
    return t * cos_t + pltpu.roll(t, RET_QK_DIM // 2, 1) * sin_t


def _mixer_kernel(x_ref, cos_ref, sin_ref, nm_ref, win_ref, cw_ref, wco_ref, rn_ref, wro_ref,
                  wo_ref, di_ref, dq_ref, dk_ref, dc_ref, *refs, tiles_per_seq):
    cast_src, o_ref, cast_dst, (state_ref, z_ref, xg_ref, h_ref, s_ref) = (
        refs[:N_CAST_MLP], refs[N_CAST_MLP], refs[N_CAST_MLP + 1:2 * N_CAST_MLP + 1],
        refs[2 * N_CAST_MLP + 1:])
    _cast_rows(cast_src, cast_dst)
    tm = TOKEN_TILE
    n_chunks = tm // CHUNK
    rows = lambda c: slice(c * CHUNK, (c + 1) * CHUNK)
    qk_cols = lambda hd: slice(hd * RET_QK_DIM, (hd + 1) * RET_QK_DIM)
    v_cols = lambda hd: slice(hd * RET_V_DIM, (hd + 1) * RET_V_DIM)
    blocks = [(hd, c) for hd in range(RET_HEADS) for c in range(n_chunks)]

    @pl.when(pl.program_id(0) % tiles_per_seq == 0)
    def _():
        state_ref[...] = jnp.zeros_like(state_ref)
        z_ref[...] = jnp.zeros_like(z_ref)

    def one_pass(r0):
        tile_rows = slice(r0, r0 + tm)
        x = x_ref[tile_rows, :]
        xg = x * nm_ref[...]
        xg_ref[...] = xg.astype(BF16)
        r = lax.rsqrt(jnp.mean(x * x, axis=-1, keepdims=True) + EPS)
        z = jnp.concatenate(
            [_dot(xg_ref[...], win_ref[:, OFF_CC + j:OFF_CC + j + CONV_BLOCK])
             * _dot(xg_ref[...], win_ref[:, OFF_CU + j:OFF_CU + j + CONV_BLOCK]) * (r * r)
             for j in range(0, D_CONV, CONV_BLOCK)], axis=1)
        h_ref[...] = (xg * r).astype(BF16)

        def proj(off, size):
            return _dot(h_ref[...], win_ref[:, off:off + size])

        q = proj(OFF_Q, RET_QK)
        k = proj(OFF_K, RET_QK)
        z_head = jnp.concatenate([z_ref[...], z[0:V7X_SUBLANES]], axis=0)
        z_ref[...] = z[tm - V7X_SUBLANES:tm]
        cw = cw_ref[...]

        def conv3(z0):
            return cw[0:1] * pltpu.roll(z0, 2, 0) + cw[1:2] * pltpu.roll(z0, 1, 0) + cw[2:3] * z0

        conv = jnp.concatenate([conv3(z_head)[V7X_SUBLANES:], conv3(z)[V7X_SUBLANES:]], axis=0)

        v = proj(OFF_V, RET_V).astype(BF16)
        cos_t = cos_ref[tile_rows, :]
        sin_t = sin_ref[tile_rows, :]

        def rotate_heads(t):
            return jnp.concatenate([_rotate(t[:, qk_cols(hd)], cos_t, sin_t) for hd in range(RET_HEADS)],
                                   axis=1)

        qr = rotate_heads(q) * (RET_QK_DIM ** -0.5)
        kr = rotate_heads(k)
        q_intra = qr.astype(BF16)
        q_inter = (qr * dq_ref[...]).astype(BF16)
        k_intra = kr.astype(BF16)
        k_state = (kr * dk_ref[...]).astype(BF16)
        g = proj(OFF_G, RET_V)

        scores = {}
        incr = {}
        for hd, c in blocks:
            scores[hd, c] = (lax.dot_general(q_intra[rows(c), qk_cols(hd)], k_intra[rows(c), qk_cols(hd)],
                                             (((1,), (1,)), ((), ())), preferred_element_type=F32)
                             * di_ref[hd]).astype(BF16)
            incr[hd, c] = lax.dot_general(k_state[rows(c), qk_cols(hd)], v[rows(c), v_cols(hd)],
                                          (((0,), (0,)), ((), ())), preferred_element_type=F32)

        s_conv = (proj(OFF_CB, D_CONV) * conv).astype(BF16)
        state_in = {}
        for hd in range(RET_HEADS):
            st = state_ref[hd]
            for c in range(n_chunks):
                state_in[hd, c] = st.astype(BF16)
                st = st * dc_ref[hd] + incr[hd, c]
            state_ref[hd] = st
        half_g = 0.5 * g
        gate = (half_g * rn_ref[...]) * (jnp.tanh(half_g) + 1.0)

        for hd, c in blocks:
            lhs = jnp.concatenate([scores[hd, c], q_inter[rows(c), qk_cols(hd)]], axis=1)
            rhs = jnp.concatenate([v[rows(c), v_cols(hd)], state_in[hd, c]], axis=0)
            o = _dot(lhs, rhs)
            mu = jnp.mean(o, axis=-1, keepdims=True)
            d = o - mu
            var = jnp.mean(d * d, axis=-1, keepdims=True)
            s_ref[rows(c), v_cols(hd)] = (d * lax.rsqrt(var + EPS) * gate[rows(c), v_cols(hd)]).astype(BF16)
        y_conv = _dot(s_conv, wco_ref[...])
        sig_a_conv = _sigmoid(proj(OFF_GA, D_MODEL)) * y_conv
        y_ret = _dot(s_ref[...], wro_ref[...])
        merged = sig_a_conv + _sigmoid(proj(OFF_GB, D_MODEL)) * y_ret
        o_ref[tile_rows, :] = x + _dot(merged.astype(BF16), wo_ref[...])

    for r0 in range(0, x_ref.shape[0], tm):
        one_pass(r0)


def _mlp_kernel(x_ref, nm_ref, wup_ref, wdn_ref, nf_ref, *refs, final_norm):
    n_cast = len(refs) // 2
    _cast_rows(refs[:n_cast], refs[n_cast + 1:])
    o_ref = refs[n_cast]
    for r0 in range(0, x_ref.shape[0], TOKEN_TILE):
        tile_rows = slice(r0, r0 + TOKEN_TILE)
        x = x_ref[tile_rows, :]
        xg = (x * nm_ref[...]).astype(BF16)
        down = None
        for j in range(0, D_FF, FF_BLOCK):
            u = _dot(xg, wup_ref[:, j:j + FF_BLOCK])
            a = jnp.square(jnp.maximum(u, 0.0)).astype(BF16)
            part = _dot(a, wdn_ref[j:j + FF_BLOCK, :])
            down = part if down is None else down + part
        r = lax.rsqrt(jnp.mean(x * x, axis=-1, keepdims=True) + EPS)
        y = x + (r * r) * down
        if final_norm:
            y = _rms(y, nf_ref[...])
        o_ref[tile_rows, :] = y


def _resident(shape, index_map):
    return pl.BlockSpec(shape, index_map, pipeline_mode=pl.Buffered(1))


def _decay_tables():
    f32 = np.float32
    log_gamma = np.log1p(-np.exp2(f32(-5.0) - np.arange(RET_HEADS, dtype=f32)))
    pos = np.arange(CHUNK, dtype=f32)
    rel = pos[:, None] - pos[None, :]
    decay_intra = np.where(rel >= 0, np.exp(log_gamma[:, None, None] * np.maximum(rel, f32(0.0))),
                           f32(0.0))
    decay_q = np.exp((pos[:, None] + f32(1.0)) * log_gamma)
    decay_k = np.exp((f32(CHUNK - 1.0) - pos[:, None]) * log_gamma)
    decay_chunk = np.exp(f32(CHUNK) * log_gamma)
    tile_layout = lambda d: np.tile(np.repeat(d, RET_QK_DIM, axis=1), (TOKEN_TILE // CHUNK, 1))
    dc = np.broadcast_to(decay_chunk[:, None, None], (RET_HEADS, 1, RET_V_DIM))
    tables = (decay_intra, tile_layout(decay_q), tile_layout(decay_k), dc)
    assert all(t.dtype == f32 for t in tables)
    return tuple(jnp.asarray(t) for t in tables)


def _cast_job(weights, layer, n_steps):
    in_specs, out_specs, out_shapes = [], [], []
    for w in weights:
        _, n_rows, n_cols = w.shape
        block_rows = n_rows // n_steps
        assert block_rows * n_steps == n_rows and block_rows % BF16_SUBLANES == 0
        in_specs.append(pl.BlockSpec((None, block_rows, n_cols), lambda i: (layer, i, 0)))
        out_specs.append(pl.BlockSpec((block_rows, n_cols), lambda i: (i, 0)))
        out_shapes.append(jax.ShapeDtypeStruct((n_rows, n_cols), BF16))
    return in_specs, out_specs, out_shapes


def _rope_tables(positions, mixer_weights):
    bsz, seq = positions.shape
    half = RET_QK_DIM // 2
    inv_freq = (ROPE_BASE ** (-jnp.arange(half, dtype=F32) / half))[:, None]
    n = bsz * seq
    tm = ROPE_TILE
    assert n % tm == 0
    cast_in, cast_out, cast_shapes = _cast_job(mixer_weights, 0, n // tm)
    cos_t, sin_t, *weights = pl.pallas_call(
        _rope_kernel,
        grid=(n // tm,),
        in_specs=[pl.BlockSpec((None, 1, tm), lambda i: (i, 0, 0)),
                  pl.BlockSpec((half, 1), lambda i: (0, 0))] + cast_in,
        out_specs=[pl.BlockSpec((tm, RET_QK_DIM), lambda i: (i, 0))] * 2 + cast_out,
        out_shape=[jax.ShapeDtypeStruct((n, RET_QK_DIM), F32)] * 2 + cast_shapes,
        name="rope_tables",
    )(positions.reshape(n // tm, 1, tm), inv_freq, *mixer_weights)
    return cos_t, sin_t, weights


def _mixer(x, cos_t, sin_t, norm_mix, conv_w, ret_norm, weights, decays, mlp_weights, layer, seq):
    n = x.shape[0]
    tm = MIXER_TILE
    assert n % tm == 0 and seq % tm == 0
    tok = lambda width: pl.BlockSpec((tm, width), lambda i: (i, 0))
    lay = lambda *shape: _resident((None,) + shape, lambda i: (layer,) + (0,) * len(shape))
    const = lambda *shape: _resident(shape, lambda i: (0,) * len(shape))
    w_in, w_conv_out, w_ret_out, w_o = weights
    cast_in, cast_out, cast_shapes = _cast_job(mlp_weights, layer, n // tm)
    out, *mlp_bf16 = pl.pallas_call(
        functools.partial(_mixer_kernel, tiles_per_seq=seq // tm),
        grid=(n // tm,),
        in_specs=[tok(D_MODEL), tok(RET_QK_DIM), tok(RET_QK_DIM),
                  lay(1, D_MODEL), const(D_MODEL, D_IN), lay(CONV_WIDTH, D_CONV),
                  const(D_CONV, D_MODEL), lay(1, RET_V), const(RET_V, D_MODEL), const(D_MODEL, D_MODEL),
                  const(RET_HEADS, CHUNK, CHUNK), const(TOKEN_TILE, RET_QK), const(TOKEN_TILE, RET_QK),
                  const(RET_HEADS, 1, RET_V_DIM)] + cast_in,
        out_specs=[tok(D_MODEL)] + cast_out,
        out_shape=[jax.ShapeDtypeStruct(x.shape, F32)] + cast_shapes,
        scratch_shapes=[pltpu.VMEM((RET_HEADS, RET_QK_DIM, RET_V_DIM), F32),
                        pltpu.VMEM((V7X_SUBLANES, D_CONV), F32),
                        pltpu.VMEM((TOKEN_TILE, D_MODEL), BF16),
                        pltpu.VMEM((TOKEN_TILE, D_MODEL), BF16),
                        pltpu.VMEM((TOKEN_TILE, RET_V), BF16)],
        compiler_params=pltpu.CompilerParams(
            dimension_semantics=("arbitrary",), vmem_limit_bytes=VMEM_LIMIT_BYTES),
        name=f"mixer_l{layer}",
    )(x, cos_t, sin_t, norm_mix, w_in, conv_w, w_conv_out, ret_norm, w_ret_out, w_o, *decays,
      *mlp_weights)
    return out, mlp_bf16


def _mlp(x, norm_mlp, weights, norm_final, next_mixer_weights, layer):
    n = x.shape[0]
    tm = MLP_TILE
    assert n % tm == 0
    tok = lambda width: pl.BlockSpec((tm, width), lambda i: (i, 0))
    const = lambda *shape: _resident(shape, lambda i: (0,) * len(shape))
    last = layer == DEPTH - 1
    cast_in, cast_out, cast_shapes = ([], [], []) if last else _cast_job(
        next_mixer_weights, layer + 1, n // tm)
    w_up, w_down = weights
    out, *mixer_bf16 = pl.pallas_call(
        functools.partial(_mlp_kernel, final_norm=last),
        grid=(n // tm,),
        in_specs=[tok(D_MODEL), _resident((None, 1, D_MODEL), lambda i: (layer, 0, 0)),
                  const(D_MODEL, D_FF), const(D_FF, D_MODEL), const(1, D_MODEL)] + cast_in,
        out_specs=[tok(D_MODEL)] + cast_out,
        out_shape=[jax.ShapeDtypeStruct(x.shape, F32)] + cast_shapes,
        compiler_params=pltpu.CompilerParams(
            dimension_semantics=("arbitrary",), vmem_limit_bytes=VMEM_LIMIT_BYTES),
        name=f"mlp_l{layer}",
    )(x, norm_mlp, w_up, w_down, norm_final, *([] if last else next_mixer_weights))
    return out, mixer_bf16


def kernel(x, positions, norm_mix, w_in, conv_w, w_conv_out, ret_norm, w_ret_out, w_o, norm_mlp, w_up, w_down, norm_final):
    bsz, seq, _ = x.shape
    assert seq % TOKEN_TILE == 0 and TOKEN_TILE % CHUNK == 0
    mixer_f32 = (w_in, w_conv_out, w_ret_out, w_o)
    mlp_f32 = (w_up, w_down)
    cos_t, sin_t, mixer_bf16 = _rope_tables(positions, mixer_f32)
    decays = _decay_tables()
    as_row = lambda a: a[:, None, :]
    norm_mix, ret_norm, norm_mlp = as_row(norm_mix), as_row(ret_norm), as_row(norm_mlp)
    norm_final = norm_final[None, :]
    x = x.reshape(bsz * seq, D_MODEL)
    for layer in range(DEPTH):
        x, mlp_bf16 = _mixer(x, cos_t, sin_t, norm_mix, conv_w, ret_norm, mixer_bf16, decays,
                             mlp_f32, layer, seq)
        x, mixer_bf16 = _mlp(x, norm_mlp, mlp_bf16, norm_final, mixer_f32, layer)
    return x.reshape(bsz, seq, D_MODEL)
```

```python
import functools

import jax
import jax.numpy as jnp
import numpy as np
from jax import lax
from jax.experimental import pallas as pl
from jax.experimental.pallas import tpu as pltpu

D_MODEL = 1024
DEPTH = 2
D_CONV = D_MODEL
CONV_WIDTH = 3
RET_HEADS = 4
RET_QK_DIM = 128
RET_V_DIM = 256
RET_QK = RET_HEADS * RET_QK_DIM
RET_V = RET_HEADS * RET_V_DIM
CHUNK = 128
ROPE_BASE = 10000.0
D_FF = 4 * D_MODEL
EPS = 1e-6
D_IN = 3 * D_CONV + 2 * RET_QK + 2 * RET_V + 2 * D_MODEL

OFF_CB = 0
OFF_CC = OFF_CB + D_CONV
OFF_CU = OFF_CC + D_CONV
OFF_Q = OFF_CU + D_CONV
OFF_K = OFF_Q + RET_QK
OFF_V = OFF_K + RET_QK
OFF_G = OFF_V + RET_V
OFF_GA = OFF_G + RET_V
OFF_GB = OFF_GA + D_MODEL

V7X_SUBLANES = 8
BF16_SUBLANES = 2 * V7X_SUBLANES
TOKEN_TILE = 512
CONV_BLOCK = 256
ROPE_TILE = 2048
MLP_TILE = 2 * TOKEN_TILE
MIXER_TILE = 2 * TOKEN_TILE
VMEM_LIMIT_BYTES = 58 * 1024 * 1024

F32 = jnp.float32
BF16 = jnp.bfloat16


def _rms(x, g):
    return x * lax.rsqrt(jnp.mean(x * x, axis=-1, keepdims=True) + EPS) * g


def _dot(a, b):
    return jnp.dot(a, b, preferred_element_type=F32)


def _sigmoid(x):
    return 0.5 * jnp.tanh(0.5 * x) + 0.5


N_CAST_MIXER = 4
N_CAST_MLP = 2


def _cast_rows(src_refs, dst_refs):
    for src, dst in zip(src_refs, dst_refs, strict=True):
        dst[...] = src[...].astype(BF16)


def _rope_kernel(pos_ref, inv_ref, *refs):
    cast_src, (cos_ref, sin_ref), cast_dst = (
        refs[:N_CAST_MIXER], refs[N_CAST_MIXER:N_CAST_MIXER + 2], refs[N_CAST_MIXER + 2:])
    _cast_rows(cast_src, cast_dst)
    ang = pos_ref[...].astype(F32) * inv_ref[...]
    cos = jnp.cos(ang)
    sin = jnp.sin(ang)
    cos_ref[...] = jnp.concatenate([cos, cos], axis=0).T
    sin_ref[...] = jnp.concatenate([-sin, sin], axis=0).T


def _rotate(t, cos_t, sin_t):
    return t * cos_t + pltpu.roll(t, RET_QK_DIM // 2, 1) * sin_t


def _mixer_kernel(x_ref, cos_ref, sin_ref, nm_ref, win_ref, cw_ref, wco_ref, rn_ref, wro_ref,
                  wo_ref, di_ref, dq_ref, dk_ref, dc_ref, *refs, tiles_per_seq):
    cast_src, o_ref, cast_dst, (state_ref, z_ref, xg_ref, h_ref, s_ref) = (
        refs[:N_CAST_MLP], refs[N_CAST_MLP], refs[N_CAST_MLP + 1:2 * N_CAST_MLP + 1],
        refs[2 * N_CAST_MLP + 1:])
    _cast_rows(cast_src, cast_dst)
    tm = TOKEN_TILE
    n_chunks = tm // CHUNK
    rows = lambda c: slice(c * CHUNK, (c + 1) * CHUNK)
    qk_cols = lambda hd: slice(hd * RET_QK_DIM, (hd + 1) * RET_QK_DIM)
    v_cols = lambda hd: slice(hd * RET_V_DIM, (hd + 1) * RET_V_DIM)
    blocks = [(hd, c) for hd in range(RET_HEADS) for c in range(n_chunks)]

    @pl.when(pl.program_id(0) % tiles_per_seq == 0)
    def _():
        state_ref[...] = jnp.zeros_like(state_ref)
        z_ref[...] = jnp.zeros_like(z_ref)

    def one_pass(r0):
        tile_rows = slice(r0, r0 + tm)
        x = x_ref[tile_rows, :]
        xg = x * nm_ref[...]
        xg_ref[...] = xg.astype(BF16)
        r = lax.rsqrt(jnp.mean(x * x, axis=-1, keepdims=True) + EPS)
        z = jnp.concatenate(
            [_dot(xg_ref[...], win_ref[:, OFF_CC + j:OFF_CC + j + CONV_BLOCK])
             * _dot(xg_ref[...], win_ref[:, OFF_CU + j:OFF_CU + j + CONV_BLOCK]) * (r * r)
             for j in range(0, D_CONV, CONV_BLOCK)], axis=1)
        h_ref[...] = (xg * r).astype(BF16)

        def proj(off, size):
            return _dot(h_ref[...], win_ref[:, off:off + size])

        qk_blocks = [(proj(OFF_Q + j, CONV_BLOCK), proj(OFF_K + j, CONV_BLOCK))
                     for j in range(0, RET_QK, CONV_BLOCK)]
        q = jnp.concatenate([qb for qb, _ in qk_blocks], axis=1)
        k = jnp.concatenate([kb for _, kb in qk_blocks], axis=1)
        z_head = jnp.concatenate([z_ref[...], z[0:V7X_SUBLANES]], axis=0)
        z_ref[...] = z[tm - V7X_SUBLANES:tm]
        cw = cw_ref[...]

        def conv3(z0):
            return cw[0:1] * pltpu.roll(z0, 2, 0) + cw[1:2] * pltpu.roll(z0, 1, 0) + cw[2:3] * z0

        conv = jnp.concatenate([conv3(z_head)[V7X_SUBLANES:], conv3(z)[V7X_SUBLANES:]], axis=0)

        v = proj(OFF_V, RET_V).astype(BF16)
        cos_t = cos_ref[tile_rows, :]
        sin_t = sin_ref[tile_rows, :]

        def rotate_heads(t):
            return jnp.concatenate([_rotate(t[:, qk_cols(hd)], cos_t, sin_t) for hd in range(RET_HEADS)],
                                   axis=1)

        qr = rotate_heads(q) * (RET_QK_DIM ** -0.5)
        kr = rotate_heads(k)
        q_intra = qr.astype(BF16)
        q_inter = (qr * dq_ref[...]).astype(BF16)
        k_intra = kr.astype(BF16)
        k_state = (kr * dk_ref[...]).astype(BF16)
        g = proj(OFF_G, RET_V)

        scores = {}
        incr = {}
        for hd, c in blocks:
            scores[hd, c] = (lax.dot_general(q_intra[rows(c), qk_cols(hd)], k_intra[rows(c), qk_cols(hd)],
                                             (((1,), (1,)), ((), ())), preferred_element_type=F32)
                             * di_ref[hd]).astype(BF16)
            incr[hd, c] = lax.dot_general(k_state[rows(c), qk_cols(hd)], v[rows(c), v_cols(hd)],
                                          (((0,), (0,)), ((), ())), preferred_element_type=F32)

        s_conv = (proj(OFF_CB, D_CONV) * conv).astype(BF16)
        state_in = {}
        for hd in range(RET_HEADS):
            st = state_ref[hd]
            for c in range(n_chunks):
                state_in[hd, c] = st.astype(BF16)
                st = st * dc_ref[hd] + incr[hd, c]
            state_ref[hd] = st
        half_g = 0.5 * g
        gate = (half_g * rn_ref[...]) * (jnp.tanh(half_g) + 1.0)

        for hd, c in blocks:
            lhs = jnp.concatenate([scores[hd, c], q_inter[rows(c), qk_cols(hd)]], axis=1)
            rhs = jnp.concatenate([v[rows(c), v_cols(hd)], state_in[hd, c]], axis=0)
            o = _dot(lhs, rhs)
            mu = jnp.mean(o, axis=-1, keepdims=True)
            d = o - mu
            var = jnp.mean(d * d, axis=-1, keepdims=True)
            s_ref[rows(c), v_cols(hd)] = (d * lax.rsqrt(var + EPS) * gate[rows(c), v_cols(hd)]).astype(BF16)
        y_conv = _dot(s_conv, wco_ref[...])
        sig_a_conv = _sigmoid(proj(OFF_GA, D_MODEL)) * y_conv
        y_ret = _dot(s_ref[...], wro_ref[...])
        merged = sig_a_conv + _sigmoid(proj(OFF_GB, D_MODEL)) * y_ret
        o_ref[tile_rows, :] = x + _dot(merged.astype(BF16), wo_ref[...])

    for r0 in range(0, x_ref.shape[0], tm):
        one_pass(r0)


def _mlp_kernel(x_ref, nm_ref, wup_ref, wdn_ref, nf_ref, *refs, final_norm):
    n_cast = len(refs) // 2
    _cast_rows(refs[:n_cast], refs[n_cast + 1:])
    o_ref = refs[n_cast]
    for r0 in range(0, x_ref.shape[0], TOKEN_TILE):
        tile_rows = slice(r0, r0 + TOKEN_TILE)
        x = x_ref[tile_rows, :]
        u = _dot((x * nm_ref[...]).astype(BF16), wup_ref[...])
        a = jnp.square(jnp.maximum(u, 0.0)).astype(BF16)
        r = lax.rsqrt(jnp.mean(x * x, axis=-1, keepdims=True) + EPS)
        y = x + (r * r) * _dot(a, wdn_ref[...])
        if final_norm:
            y = _rms(y, nf_ref[...])
        o_ref[tile_rows, :] = y


def _resident(shape, index_map):
    return pl.BlockSpec(shape, index_map, pipeline_mode=pl.Buffered(1))


def _decay_tables():
    f32 = np.float32
    log_gamma = np.log1p(-np.exp2(f32(-5.0) - np.arange(RET_HEADS, dtype=f32)))
    pos = np.arange(CHUNK, dtype=f32)
    rel = pos[:, None] - pos[None, :]
    decay_intra = np.where(rel >= 0, np.exp(log_gamma[:, None, None] * np.maximum(rel, f32(0.0))),
                           f32(0.0))
    decay_q = np.exp((pos[:, None] + f32(1.0)) * log_gamma)
    decay_k = np.exp((f32(CHUNK - 1.0) - pos[:, None]) * log_gamma)
    decay_chunk = np.exp(f32(CHUNK) * log_gamma)
    tile_layout = lambda d: np.tile(np.repeat(d, RET_QK_DIM, axis=1), (TOKEN_TILE // CHUNK, 1))
    dc = np.broadcast_to(decay_chunk[:, None, None], (RET_HEADS, 1, RET_V_DIM))
    tables = (decay_intra, tile_layout(decay_q), tile_layout(decay_k), dc)
    assert all(t.dtype == f32 for t in tables)
    return tuple(jnp.asarray(t) for t in tables)


def _cast_job(weights, layer, n_steps):
    in_specs, out_specs, out_shapes = [], [], []
    for w in weights:
        _, n_rows, n_cols = w.shape
        block_rows = n_rows // n_steps
        assert block_rows * n_steps == n_rows and block_rows % BF16_SUBLANES == 0
        in_specs.append(pl.BlockSpec((None, block_rows, n_cols), lambda i: (layer, i, 0)))
        out_specs.append(pl.BlockSpec((block_rows, n_cols), lambda i: (i, 0)))
        out_shapes.append(jax.ShapeDtypeStruct((n_rows, n_cols), BF16))
    return in_specs, out_specs, out_shapes


def _rope_tables(positions, mixer_weights):
    bsz, seq = positions.shape
    half = RET_QK_DIM // 2
    inv_freq = (ROPE_BASE ** (-jnp.arange(half, dtype=F32) / half))[:, None]
    n = bsz * seq
    tm = ROPE_TILE
    assert n % tm == 0
    cast_in, cast_out, cast_shapes = _cast_job(mixer_weights, 0, n // tm)
    cos_t, sin_t, *weights = pl.pallas_call(
        _rope_kernel,
        grid=(n // tm,),
        in_specs=[pl.BlockSpec((None, 1, tm), lambda i: (i, 0, 0)),
                  pl.BlockSpec((half, 1), lambda i: (0, 0))] + cast_in,
        out_specs=[pl.BlockSpec((tm, RET_QK_DIM), lambda i: (i, 0))] * 2 + cast_out,
        out_shape=[jax.ShapeDtypeStruct((n, RET_QK_DIM), F32)] * 2 + cast_shapes,
        name="rope_tables",
    )(positions.reshape(n // tm, 1, tm), inv_freq, *mixer_weights)
    return cos_t, sin_t, weights


def _mixer(x, cos_t, sin_t, norm_mix, conv_w, ret_norm, weights, decays, mlp_weights, layer, seq):
    n = x.shape[0]
    tm = MIXER_TILE
    assert n % tm == 0 and seq % tm == 0
    tok = lambda width: pl.BlockSpec((tm, width), lambda i: (i, 0))
    lay = lambda *shape: _resident((None,) + shape, lambda i: (layer,) + (0,) * len(shape))
    const = lambda *shape: _resident(shape, lambda i: (0,) * len(shape))
    w_in, w_conv_out, w_ret_out, w_o = weights
    cast_in, cast_out, cast_shapes = _cast_job(mlp_weights, layer, n // tm)
    out, *mlp_bf16 = pl.pallas_call(
        functools.partial(_mixer_kernel, tiles_per_seq=seq // tm),
        grid=(n // tm,),
        in_specs=[tok(D_MODEL), tok(RET_QK_DIM), tok(RET_QK_DIM),
                  lay(1, D_MODEL), const(D_MODEL, D_IN), lay(CONV_WIDTH, D_CONV),
                  const(D_CONV, D_MODEL), lay(1, RET_V), const(RET_V, D_MODEL), const(D_MODEL, D_MODEL),
                  const(RET_HEADS, CHUNK, CHUNK), const(TOKEN_TILE, RET_QK), const(TOKEN_TILE, RET_QK),
                  const(RET_HEADS, 1, RET_V_DIM)] + cast_in,
        out_specs=[tok(D_MODEL)] + cast_out,
        out_shape=[jax.ShapeDtypeStruct(x.shape, F32)] + cast_shapes,
        scratch_shapes=[pltpu.VMEM((RET_HEADS, RET_QK_DIM, RET_V_DIM), F32),
                        pltpu.VMEM((V7X_SUBLANES, D_CONV), F32),
                        pltpu.VMEM((TOKEN_TILE, D_MODEL), BF16),
                        pltpu.VMEM((TOKEN_TILE, D_MODEL), BF16),
                        pltpu.VMEM((TOKEN_TILE, RET_V), BF16)],
        compiler_params=pltpu.CompilerParams(
            dimension_semantics=("arbitrary",), vmem_limit_bytes=VMEM_LIMIT_BYTES),
        name=f"mixer_l{layer}",
    )(x, cos_t, sin_t, norm_mix, w_in, conv_w, w_conv_out, ret_norm, w_ret_out, w_o, *decays,
      *mlp_weights)
    return out, mlp_bf16


def _mlp(x, norm_mlp, weights, norm_final, next_mixer_weights, layer):
    n = x.shape[0]
    tm = MLP_TILE
    assert n % tm == 0
    tok = lambda width: pl.BlockSpec((tm, width), lambda i: (i, 0))
    const = lambda *shape: _resident(shape, lambda i: (0,) * len(shape))
    last = layer == DEPTH - 1
    cast_in, cast_out, cast_shapes = ([], [], []) if last else _cast_job(
        next_mixer_weights, layer + 1, n // tm)
    w_up, w_down = weights
    out, *mixer_bf16 = pl.pallas_call(
        functools.partial(_mlp_kernel, final_norm=last),
        grid=(n // tm,),
        in_specs=[tok(D_MODEL), _resident((None, 1, D_MODEL), lambda i: (layer, 0, 0)),
                  const(D_MODEL, D_FF), const(D_FF, D_MODEL), const(1, D_MODEL)] + cast_in,
        out_specs=[tok(D_MODEL)] + cast_out,
        out_shape=[jax.ShapeDtypeStruct(x.shape, F32)] + cast_shapes,
        compiler_params=pltpu.CompilerParams(
            dimension_semantics=("arbitrary",), vmem_limit_bytes=VMEM_LIMIT_BYTES),
        name=f"mlp_l{layer}",
    )(x, norm_mlp, w_up, w_down, norm_final, *([] if last else next_mixer_weights))
    return out, mixer_bf16


def kernel(x, positions, norm_mix, w_in, conv_w, w_conv_out, ret_norm, w_ret_out, w_o, norm_mlp, w_up, w_down, norm_final):
    bsz, seq, _ = x.shape
    assert seq % TOKEN_TILE == 0 and TOKEN_TILE % CHUNK == 0
    mixer_f32 = (w_in, w_conv_out, w_ret_out, w_o)
    mlp_f32 = (w_up, w_down)
    cos_t, sin_t, mixer_bf16 = _rope_tables(positions, mixer_f32)
    decays = _decay_tables()
    as_row = lambda a: a[:, None, :]
    norm_mix, ret_norm, norm_mlp = as_row(norm_mix), as_row(ret_norm), as_row(norm_mlp)
    norm_final = norm_final[None, :]
    x = x.reshape(bsz * seq, D_MODEL)
    for layer in range(DEPTH):
        x, mlp_bf16 = _mixer(x, cos_t, sin_t, norm_mix, conv_w, ret_norm, mixer_bf16, decays,
                             mlp_f32, layer, seq)
        x, mixer_bf16 = _mlp(x, norm_mlp, mlp_bf16, norm_final, mixer_f32, layer)
    return x.reshape(bsz, seq, D_MODEL)
```
